```python
import functools
import jax, jax.numpy as jnp
from jax import lax
import numpy as np

D_MODEL = 2048
BATCH = 4
SEQ = 4096
DEPTH = 1
DEC_BATCH = 32
DEC_SEQ = 1
PAST_LEN = 16384
PAGE_SIZE = 128

HEAD_DIM = 128
N_HEADS = 8
MEM_HEADS = 4
MEM_TOKENS = 256
POOL_WINDOWS = (2, 4, 8, 16)
POOL_GROUPS = len(POOL_WINDOWS)
POOL_GROUP_DIM = 128
D_POOL = POOL_GROUPS * POOL_GROUP_DIM
POOL_STATE = max(POOL_WINDOWS) - 1
D_ATTN = N_HEADS * HEAD_DIM
D_MEM = MEM_HEADS * HEAD_DIM
N_BRANCH = 3
SPLIT_POINTS = (D_POOL, D_POOL + D_ATTN, D_POOL + 2 * D_ATTN, D_POOL + 3 * D_ATTN,
                D_POOL + 3 * D_ATTN + D_MEM)
D_IN = D_POOL + 3 * D_ATTN + D_MEM + N_BRANCH * D_MODEL
MOBA_BLOCK = 256
MOBA_TOPK = 3
MOBA_Q_CHUNK = 32
PAGES_PER_BLOCK = MOBA_BLOCK // PAGE_SIZE
ROPE_THETA = 500000.0
ROPE_DIM = HEAD_DIM // 4
D_FF = 5632
CONV_WIDTH = 3
EPS = 1e-6
NEG = -1e30
SCALE = HEAD_DIM ** -0.5
F32 = jnp.float32

kernel_name = 'pool_moba_mem_convffn_step'


def rms_norm(x, g):
    xf = x.astype(F32)
    y = xf * lax.rsqrt(jnp.mean(xf * xf, axis=-1, keepdims=True) + EPS)
    return (y * g.astype(F32)).astype(x.dtype)


def rope_partial(x, pos):
    half = ROPE_DIM // 2
    inv = ROPE_THETA ** (-jnp.arange(half, dtype=F32) * (2.0 / ROPE_DIM))
    ang = pos.astype(F32)[:, None] * inv[None, :]
    cos, sin = jnp.cos(ang), jnp.sin(ang)
    xf = x.astype(F32)
    x1, x2 = xf[..., :half], xf[..., half:ROPE_DIM]
    out = jnp.concatenate([x1 * cos - x2 * sin, x2 * cos + x1 * sin, xf[..., ROPE_DIM:]], axis=-1)
    return out.astype(x.dtype)


def to_heads(t, n):
    b, l, _ = t.shape
    return t.reshape(b, l, n, HEAD_DIM).transpose(0, 2, 1, 3)


def pool_mix(ext, pos, pool_w, pool_scale):
    b, n_ext, _ = ext.shape
    L = n_ext - POOL_STATE
    xf = ext.astype(F32)
    cs = jnp.concatenate([jnp.zeros_like(xf[:, :1]), jnp.cumsum(xf, axis=1)], axis=1)
    end = cs[:, POOL_STATE + 1:]
    cur = xf[:, POOL_STATE:]
    outs = []
    for gi, w in enumerate(POOL_WINDOWS):
        sl = slice(gi * POOL_GROUP_DIM, (gi + 1) * POOL_GROUP_DIM)
        start = cs[:, POOL_STATE + 1 - w: POOL_STATE + 1 - w + L, sl]
        cnt = jnp.minimum(w, pos + 1).astype(F32)[None, :, None]
        outs.append((end[..., sl] - start) / cnt - cur[..., sl])
    p = jnp.stack(outs, axis=2)
    y = jnp.einsum('blgc,gcd->blgd', p, pool_w.astype(F32)).reshape(b, L, D_POOL)
    return (y * pool_scale.astype(F32)).astype(ext.dtype)


def mem_kv(mem, w_mem_kv, mem_k_norm_g):
    bm, m, _ = mem.shape
    k, v = jnp.split(mem @ w_mem_kv, 2, axis=-1)
    k = rms_norm(k.reshape(bm, m, MEM_HEADS, HEAD_DIM), mem_k_norm_g)
    return k, v.reshape(bm, m, MEM_HEADS, HEAD_DIM)


def mem_attend(qm, mem_k, mem_v, mem_q_norm_g):
    b, L, _ = qm.shape
    q = rms_norm(qm.reshape(b, L, MEM_HEADS, HEAD_DIM), mem_q_norm_g)
    s = jnp.einsum('blhd,bmhd->bhlm', q, mem_k.astype(q.dtype), preferred_element_type=F32) * SCALE
    p = jax.nn.softmax(s, axis=-1).astype(mem_v.dtype)
    o = jnp.einsum('bhlm,bmhd->blhd', p, mem_v)
    return o.reshape(b, L, D_MEM).astype(qm.dtype)


def moba_prompt(q, k, v):
    b, h, s, _ = q.shape
    nbf = -(-s // MOBA_BLOCK)
    pad = nbf * MOBA_BLOCK - s
    k_pad = jnp.pad(k, ((0, 0), (0, 0), (0, pad), (0, 0)))
    v_pad = jnp.pad(v, ((0, 0), (0, 0), (0, pad), (0, 0)))
    kb = k_pad.reshape(b, h, nbf, MOBA_BLOCK, HEAD_DIM)
    vb = v_pad.reshape(b, h, nbf, MOBA_BLOCK, HEAD_DIM)
    kmean = jnp.sum(kb.reshape(b, h, nbf, PAGES_PER_BLOCK, PAGE_SIZE, HEAD_DIM), axis=4,
                    dtype=F32).sum(axis=3) / MOBA_BLOCK
    k_sel = min(MOBA_TOPK, nbf - 1)
    n_chunks = s // MOBA_Q_CHUNK
    q_chunks = q.reshape(b, h, n_chunks, MOBA_Q_CHUNK, HEAD_DIM).transpose(2, 0, 1, 3, 4)
    starts = jnp.arange(n_chunks, dtype=jnp.int32) * MOBA_Q_CHUNK
    bi = jnp.arange(b)[:, None, None, None]
    hi = jnp.arange(h)[None, :, None, None]

    def step(args):
        qq, c0 = args
        blk = c0 // MOBA_BLOCK
        qpos = c0 + jnp.arange(MOBA_Q_CHUNK)
        ko = lax.dynamic_slice_in_dim(k_pad, blk * MOBA_BLOCK, MOBA_BLOCK, axis=2)
        vo = lax.dynamic_slice_in_dim(v_pad, blk * MOBA_BLOCK, MOBA_BLOCK, axis=2)
        kpos = blk * MOBA_BLOCK + jnp.arange(MOBA_BLOCK)
        s_own = jnp.einsum('bhqd,bhkd->bhqk', qq, ko, preferred_element_type=F32) * SCALE
        s_own = jnp.where(kpos[None, :] <= qpos[:, None], s_own, NEG)
        if k_sel == 0:
            p = jax.nn.softmax(s_own, axis=-1).astype(v.dtype)
            return jnp.einsum('bhqk,bhkd->bhqd', p, vo)
        gs = jnp.einsum('bhqd,bhnd->bhqn', qq.astype(F32), kmean)
        gs = jnp.where(jnp.arange(nbf) < blk, gs, NEG)
        _, idx = lax.top_k(gs, k_sel)
        ks = kb[bi, hi, idx]
        vs = vb[bi, hi, idx]
        s_sel = jnp.einsum('bhqd,bhqnkd->bhqnk', qq, ks, preferred_element_type=F32) * SCALE
        s_sel = jnp.where((jnp.arange(k_sel) < blk)[:, None], s_sel, NEG)
        n_sel = k_sel * MOBA_BLOCK
        sc = jnp.concatenate([s_sel.reshape(b, h, MOBA_Q_CHUNK, n_sel), s_own], axis=-1)
        p = jax.nn.softmax(sc, axis=-1).astype(v.dtype)
        o = (jnp.einsum('bhqnk,bhqnkd->bhqd', p[..., :n_sel].reshape(b, h, MOBA_Q_CHUNK, k_sel, MOBA_BLOCK),
                        vs, preferred_element_type=F32)
             + jnp.einsum('bhqk,bhkd->bhqd', p[..., n_sel:], vo, preferred_element_type=F32))
        return o.astype(v.dtype)

    out = lax.map(step, (q_chunks, starts))
    return out.transpose(1, 2, 0, 3, 4).reshape(b, h, s, HEAD_DIM)


def moba_sample(q, k_new, v_new, *, page_sums, cache_k, cache_v, layer, page_table):
    db, h, t, _ = q.shape
    n_pages = page_table.shape[1]
    past_len = n_pages * PAGE_SIZE
    n_past_blocks = past_len // MOBA_BLOCK
    own_page0 = n_past_blocks * PAGES_PER_BLOCK
    own_start = n_past_blocks * MOBA_BLOCK
    qpos = past_len + jnp.arange(t)
    if n_pages > own_page0:
        pts = page_table[:, own_page0:]
        ko_past = cache_k[layer, pts].transpose(0, 2, 1, 3, 4).reshape(db, h, -1, HEAD_DIM)
        vo_past = cache_v[layer, pts].transpose(0, 2, 1, 3, 4).reshape(db, h, -1, HEAD_DIM)
        ko = jnp.concatenate([ko_past.astype(k_new.dtype), k_new], axis=2)
        vo = jnp.concatenate([vo_past.astype(v_new.dtype), v_new], axis=2)
    else:
        ko, vo = k_new, v_new
    kpos = own_start + jnp.arange(ko.shape[2])
    s_own = jnp.einsum('bhqd,bhkd->bhqk', q, ko, preferred_element_type=F32) * SCALE
    s_own = jnp.where(kpos[None, :] <= qpos[:, None], s_own, NEG)
    k_sel = min(MOBA_TOPK, n_past_blocks)
    if k_sel == 0:
        p = jax.nn.softmax(s_own, axis=-1).astype(vo.dtype)
        return jnp.einsum('bhqk,bhkd->bhqd', p, vo)
    pt = page_table[:, :own_page0]
    kmean = page_sums[pt].reshape(db, n_past_blocks, PAGES_PER_BLOCK, h, HEAD_DIM).sum(axis=2) / MOBA_BLOCK
    gs = jnp.einsum('bhqd,bnhd->bhqn', q.astype(F32), kmean)
    _, idx = lax.top_k(gs, k_sel)
    bidx = jnp.arange(db)[:, None, None, None, None]
    hidx = jnp.arange(h)[None, :, None, None, None]
    logical = idx[..., None] * PAGES_PER_BLOCK + jnp.arange(PAGES_PER_BLOCK)
    phys = page_table[bidx, logical]
    n_sel = k_sel * MOBA_BLOCK
    ks = cache_k[layer, phys, hidx].reshape(db, h, t, n_sel, HEAD_DIM).astype(q.dtype)
    vs = cache_v[layer, phys, hidx].reshape(db, h, t, n_sel, HEAD_DIM).astype(v_new.dtype)
    s_sel = jnp.einsum('bhqd,bhqkd->bhqk', q, ks, preferred_element_type=F32) * SCALE
    p = jax.nn.softmax(jnp.concatenate([s_sel, s_own], axis=-1), axis=-1).astype(v_new.dtype)
    o = (jnp.einsum('bhqk,bhqkd->bhqd', p[..., :n_sel], vs, preferred_element_type=F32)
         + jnp.einsum('bhqk,bhkd->bhqd', p[..., n_sel:], vo, preferred_element_type=F32))
    return o.astype(v_new.dtype)


def conv_ffn(x, conv_prev, norm2_g, w_up, conv_w, conv_b, w_down):
    up = rms_norm(x, norm2_g) @ w_up
    L = up.shape[1]
    ext = jnp.concatenate([conv_prev.astype(up.dtype), up], axis=1)
    c = conv_b
    for j in range(CONV_WIDTH):
        c = c + conv_w[j] * ext[:, j:j + L]
    a, g = jnp.split(c, 2, axis=-1)
    y = (jax.nn.silu(a) * g) @ w_down
    return x + y, ext[:, L:]


def layer_forward(x, pos, pool_prev, conv_prev, mem_k, mem_v, attend,
                  norm1_g, w_in, q_norm_g, k_norm_g, mem_q_norm_g, pool_w, pool_scale,
                  w_branch_pool, w_branch_attn, w_branch_mem, w_out,
                  norm2_g, w_up, conv_w, conv_b, w_down):
    b, L, _ = x.shape
    h = rms_norm(x, norm1_g)
    u, q, k, v, qm, gate_logits = jnp.split(h @ w_in, SPLIT_POINTS, axis=-1)
    pool_ext = jnp.concatenate([pool_prev.astype(u.dtype), u], axis=1)
    y_pool = pool_mix(pool_ext, pos, pool_w, pool_scale)
    q = rope_partial(rms_norm(to_heads(q, N_HEADS), q_norm_g), pos)
    k = rope_partial(rms_norm(to_heads(k, N_HEADS), k_norm_g), pos)
    v = to_heads(v, N_HEADS)
    y_attn = attend(q, k, v).transpose(0, 2, 1, 3).reshape(b, L, D_ATTN)
    y_mem = mem_attend(qm, mem_k, mem_v, mem_q_norm_g)
    g_pool, g_attn, g_mem = jnp.split(jax.nn.sigmoid(gate_logits.astype(F32)).astype(x.dtype), N_BRANCH, axis=-1)
    merged = g_pool * (y_pool @ w_branch_pool) + g_attn * (y_attn @ w_branch_attn) + g_mem * (y_mem @ w_branch_mem)
    x = x + merged @ w_out
    x, conv_state = conv_ffn(x, conv_prev, norm2_g, w_up, conv_w, conv_b, w_down)
    return x, k, v, pool_ext[:, -POOL_STATE:], conv_state


def setup_inputs(seed: int = 0) -> dict:
    key = jax.random.key(seed)
    ks = jax.random.split(key, 32)
    n_pages = PAST_LEN // PAGE_SIZE
    n_used = DEC_BATCH * n_pages
    n_pool = n_used + n_used // 4

    def nrm(k, shape, scale=1.0):
        return jax.random.normal(k, shape, F32) * scale

    def gain(k, shape):
        return 1.0 + 0.05 * jax.random.normal(k, shape, F32)

    page_table = jax.random.permutation(ks[5], n_pool)[:n_used].astype(jnp.int32).reshape(DEC_BATCH, n_pages)
    return {
        'x_prompt': nrm(ks[0], (BATCH, SEQ, D_MODEL)),
        'x_sample': nrm(ks[1], (DEC_BATCH, DEC_SEQ, D_MODEL)),
        'mem_prompt': nrm(ks[2], (BATCH, MEM_TOKENS, D_MODEL)),
        'cache_k': nrm(ks[3], (DEPTH, n_pool, N_HEADS, PAGE_SIZE, HEAD_DIM)),
        'cache_v': nrm(ks[4], (DEPTH, n_pool, N_HEADS, PAGE_SIZE, HEAD_DIM)),
        'page_table': page_table,
        'cache_mem_k': nrm(ks[6], (DEPTH, DEC_BATCH, MEM_TOKENS, MEM_HEADS, HEAD_DIM)),
        'cache_mem_v': nrm(ks[7], (DEPTH, DEC_BATCH, MEM_TOKENS, MEM_HEADS, HEAD_DIM)),
        'state_pool': nrm(ks[8], (DEPTH, DEC_BATCH, POOL_STATE, D_POOL)),
        'state_conv': nrm(ks[9], (DEPTH, DEC_BATCH, CONV_WIDTH - 1, 2 * D_FF)),
        'norm1_g': gain(ks[10], (DEPTH, D_MODEL)),
        'w_in': nrm(ks[11], (DEPTH, D_MODEL, D_IN), D_MODEL ** -0.5),
        'q_norm_g': gain(ks[12], (DEPTH, HEAD_DIM)),
        'k_norm_g': gain(ks[13], (DEPTH, HEAD_DIM)),
        'mem_q_norm_g': gain(ks[14], (DEPTH, HEAD_DIM)),
        'mem_k_norm_g': gain(ks[15], (DEPTH, HEAD_DIM)),
        'w_mem_kv': nrm(ks[16], (DEPTH, D_MODEL, 2 * D_MEM), D_MODEL ** -0.5),
        'pool_w': nrm(ks[17], (DEPTH, POOL_GROUPS, POOL_GROUP_DIM, POOL_GROUP_DIM), POOL_GROUP_DIM ** -0.5),
        'pool_scale': gain(ks[18], (DEPTH, D_POOL)),
        'w_branch_pool': nrm(ks[19], (DEPTH, D_POOL, D_MODEL), D_POOL ** -0.5),
        'w_branch_attn': nrm(ks[20], (DEPTH, D_ATTN, D_MODEL), D_ATTN ** -0.5),
        'w_branch_mem': nrm(ks[21], (DEPTH, D_MEM, D_MODEL), D_MEM ** -0.5),
        'w_out': nrm(ks[22], (DEPTH, D_MODEL, D_MODEL), 0.5 * D_MODEL ** -0.5),
        'norm2_g': gain(ks[23], (DEPTH, D_MODEL)),
        'w_up': nrm(ks[24], (DEPTH, D_MODEL, 2 * D_FF), D_MODEL ** -0.5),
        'conv_w': nrm(ks[25], (DEPTH, CONV_WIDTH, 2 * D_FF), CONV_WIDTH ** -0.5),
        'conv_b': nrm(ks[26], (DEPTH, 2 * D_FF), 0.02),
        'w_down': nrm(ks[27], (DEPTH, D_FF, D_MODEL), 0.5 * D_FF ** -0.5),
    }


def reference(x_prompt, x_sample, mem_prompt, cache_k, cache_v, page_table, cache_mem_k, cache_mem_v,
              state_pool, state_conv, norm1_g, w_in, q_norm_g, k_norm_g, mem_q_norm_g, mem_k_norm_g,
              w_mem_kv, pool_w, pool_scale, w_branch_pool, w_branch_attn, w_branch_mem, w_out,
              norm2_g, w_up, conv_w, conv_b, w_down):
    b, s, _ = x_prompt.shape
    t = x_sample.shape[1]
    assert t <= PAGE_SIZE
    past_len = page_table.shape[1] * PAGE_SIZE
    pos_p = jnp.arange(s, dtype=jnp.int32)
    pos_s = past_len + jnp.arange(t, dtype=jnp.int32)
    zeros_pool = jnp.zeros((b, POOL_STATE, D_POOL), x_prompt.dtype)
    zeros_conv = jnp.zeros((b, CONV_WIDTH - 1, 2 * D_FF), x_prompt.dtype)
    page_sums = jnp.sum(cache_k, axis=3, dtype=F32)
    y_prompt, y_sample = x_prompt, x_sample
    kp_l, vp_l, ks_l, vs_l, mkp_l, mvp_l, pp_l, ps_l, cp_l, cs_l = ([] for _ in range(10))
    for l in range(DEPTH):
        w = (norm1_g[l], w_in[l], q_norm_g[l], k_norm_g[l], mem_q_norm_g[l], pool_w[l], pool_scale[l],
             w_branch_pool[l], w_branch_attn[l], w_branch_mem[l], w_out[l],
             norm2_g[l], w_up[l], conv_w[l], conv_b[l], w_down[l])
        mk_p, mv_p = mem_kv(mem_prompt, w_mem_kv[l], mem_k_norm_g[l])
        y_prompt, kp, vp, pool_p, conv_p = layer_forward(
            y_prompt, pos_p, zeros_pool, zeros_conv, mk_p, mv_p, moba_prompt, *w)
        attend_s = functools.partial(moba_sample, page_sums=page_sums[l], cache_k=cache_k, cache_v=cache_v,
                                     layer=l, page_table=page_table)
        y_sample, ks_new, vs_new, pool_s, conv_s = layer_forward(
            y_sample, pos_s, state_pool[l], state_conv[l], cache_mem_k[l], cache_mem_v[l], attend_s, *w)
        kp_l.append(kp); vp_l.append(vp); ks_l.append(ks_new); vs_l.append(vs_new)
        mkp_l.append(mk_p); mvp_l.append(mv_p); pp_l.append(pool_p); ps_l.append(pool_s)
        cp_l.append(conv_p); cs_l.append(conv_s)
    return (y_prompt, y_sample, jnp.stack(kp_l), jnp.stack(vp_l), jnp.stack(ks_l), jnp.stack(vs_l),
            jnp.stack(mkp_l), jnp.stack(mvp_l), jnp.stack(pp_l), jnp.stack(ps_l),
            jnp.stack(cp_l), jnp.stack(cs_l))
```

```python
import functools

import jax
import jax.numpy as jnp
import numpy as np
from jax import lax
from jax.experimental import pallas as pl
from jax.experimental.pallas import tpu as pltpu

F32 = jnp.float32
BF16 = jnp.bfloat16

D_MODEL = 2048
HEAD_DIM = 128
N_HEADS = 8
MEM_HEADS = 4
POOL_WINDOWS = (2, 4, 8, 16)
POOL_GROUP_DIM = 128
D_POOL = len(POOL_WINDOWS) * POOL_GROUP_DIM
POOL_STATE = max(POOL_WINDOWS) - 1
D_ATTN = N_HEADS * HEAD_DIM
D_MEM = MEM_HEADS * HEAD_DIM
D_QKV = D_POOL + 3 * D_ATTN + D_MEM
N_BRANCH = 3
MOBA_BLOCK = 256
MOBA_TOPK = 3
PAGE_SIZE = 128
PAGES_PER_BLOCK = MOBA_BLOCK // PAGE_SIZE
ROPE_THETA = 500000.0
ROPE_DIM = HEAD_DIM // 4
ROPE_HALF = ROPE_DIM // 2
D_FF = 5632
CONV_WIDTH = 3
EPS = 1e-6
NEG = -1e30
SCALE = HEAD_DIM ** -0.5

LANES = 128
SUBLANES = 8
BF16_ROWS = 2 * SUBLANES
POOL_HALO = 16
CONV_HALO = SUBLANES
VMEM_LIMIT = 56 * 1024 * 1024
FFN_CHUNK = 512
MERGE_CHUNK = 512
PAGE_RING = 4


def _params(*sem):
    return pltpu.CompilerParams(dimension_semantics=sem, vmem_limit_bytes=VMEM_LIMIT)


def _rms(x, g):
    ms = jnp.mean(x * x, axis=-1, keepdims=True)
    return x * lax.rsqrt(ms + EPS) * g


def _rope(t, cos, sin_lo, sin_hi):
    return (t * cos + pltpu.roll(t, ROPE_HALF, 1) * sin_hi
            + pltpu.roll(t, HEAD_DIM - ROPE_HALF, 1) * sin_lo)


def _rope_tables(pos):
    half = ROPE_HALF
    inv = ROPE_THETA ** (-jnp.arange(half, dtype=F32) * (2.0 / ROPE_DIM))
    ang = pos.astype(F32)[:, None] * inv[None, :]
    cos, sin = jnp.cos(ang), jnp.sin(ang)
    n = pos.shape[0]
    pad = jnp.zeros((n, HEAD_DIM - ROPE_DIM), F32)
    zero = jnp.zeros((n, half), F32)
    cos_t = jnp.concatenate([cos, cos, pad + 1.0], axis=1)
    sin_lo = jnp.concatenate([-sin, zero, pad], axis=1)
    sin_hi = jnp.concatenate([zero, sin, pad], axis=1)
    return cos_t, sin_lo, sin_hi


def _dot(a, b):
    return jnp.dot(a, b, preferred_element_type=F32)


def _dot_t(a, b):
    return lax.dot_general(a, b, (((1,), (1,)), ((), ())), preferred_element_type=F32)


def _top3(gs, n_valid):
    lane = lax.broadcasted_iota(jnp.int32, gs.shape, 1)
    out = []
    for r in range(MOBA_TOPK):
        mx = jnp.max(gs, axis=-1, keepdims=True)
        idx = jnp.min(jnp.where(gs == mx, lane, LANES), axis=-1, keepdims=True)
        out.append(jnp.where(r < n_valid, idx, -1))
        gs = jnp.where(lane == idx, -jnp.inf, gs)
    return out


def _pool_branch(u, win_sum, cnt, pw_ref, pscale_ref, yp_ref):
    for g, w in enumerate(POOL_WINDOWS):
        sl = slice(g * POOL_GROUP_DIM, (g + 1) * POOL_GROUP_DIM)
        p = win_sum(g, w) / cnt(w) - u[:, sl]
        y = _dot(p.astype(BF16), pw_ref[g]) * pscale_ref[:, sl]
        yp_ref[:, sl] = y.astype(yp_ref.dtype)


def _inproj_prompt_kernel(x_ref, g1_ref, w_ref, cos_ref, slo_ref, shi_ref, qg_ref, kg_ref, mqg_ref,
                          pw_ref, pscale_ref,
                          u_ref, q_ref, k_ref, v_ref, qm_ref, yp_ref,
                          uext_ref, *, tm, tiles_per_seq):
    i = pl.program_id(0)
    t_in_seq = i % tiles_per_seq
    hb = _rms(x_ref[...], g1_ref[...]).astype(BF16)

    u = _dot(hb, w_ref[:, 0:D_POOL])
    u_ref[...] = u

    @pl.when(t_in_seq == 0)
    def _():
        uext_ref[0:POOL_HALO, :] = jnp.zeros((POOL_HALO, D_POOL), F32)

    uext_ref[POOL_HALO:POOL_HALO + tm, :] = u
    pos = t_in_seq * tm + lax.broadcasted_iota(jnp.int32, (tm, 1), 0)

    def win_sum(g, w):
        sl = slice(g * POOL_GROUP_DIM, (g + 1) * POOL_GROUP_DIM)
        s = uext_ref[POOL_HALO:POOL_HALO + tm, sl]
        for d in range(1, w):
            s = s + uext_ref[POOL_HALO - d:POOL_HALO - d + tm, sl]
        return s

    _pool_branch(u, win_sum, lambda w: jnp.minimum(w, pos + 1).astype(F32), pw_ref, pscale_ref, yp_ref)
    uext_ref[0:POOL_HALO, :] = uext_ref[tm:tm + POOL_HALO, :]

    cos, slo, shi = cos_ref[...], slo_ref[...], shi_ref[...]
    c0 = D_POOL
    qf = _dot(hb, w_ref[:, c0:c0 + D_ATTN])
    for h in range(N_HEADS):
        t = _rms(qf[:, h * HEAD_DIM:(h + 1) * HEAD_DIM], qg_ref[...])
        q_ref[0, h] = _rope(t, cos, slo, shi)
    c0 += D_ATTN
    kf = _dot(hb, w_ref[:, c0:c0 + D_ATTN])
    for h in range(N_HEADS):
        t = _rms(kf[:, h * HEAD_DIM:(h + 1) * HEAD_DIM], kg_ref[...])
        k_ref[0, h] = _rope(t, cos, slo, shi)
    c0 += D_ATTN
    vf = _dot(hb, w_ref[:, c0:c0 + D_ATTN])
    for h in range(N_HEADS):
        v_ref[0, h] = vf[:, h * HEAD_DIM:(h + 1) * HEAD_DIM]
    c0 += D_ATTN
    mf = _dot(hb, w_ref[:, c0:c0 + D_MEM])
    for h in range(MEM_HEADS):
        sl = slice(h * HEAD_DIM, (h + 1) * HEAD_DIM)
        qm_ref[:, sl] = _rms(mf[:, sl], mqg_ref[...]).astype(qm_ref.dtype)


def _inproj_prompt(x2d, g1, w_in_b, tables, qg, kg, mqg, pool_w_b, pool_scale, *, batch, seq, tm):
    n = x2d.shape[0]
    tiles_per_seq = seq // tm
    row = lambda i: (i, 0)
    const = lambda i: (0, 0)
    tab = pl.BlockSpec((tm, HEAD_DIM), lambda i: (i % tiles_per_seq, 0))
    head_major = pl.BlockSpec((1, N_HEADS, tm, HEAD_DIM), lambda i: (i // tiles_per_seq, 0, i % tiles_per_seq, 0))
    hm_shape = jax.ShapeDtypeStruct((batch, N_HEADS, seq, HEAD_DIM), F32)
    return pl.pallas_call(
        functools.partial(_inproj_prompt_kernel, tm=tm, tiles_per_seq=tiles_per_seq),
        grid=(n // tm,),
        in_specs=[
            pl.BlockSpec((tm, D_MODEL), row),
            pl.BlockSpec((1, D_MODEL), const),
            pl.BlockSpec((D_MODEL, D_QKV), const, pipeline_mode=pl.Buffered(1)),
            tab, tab, tab,
            pl.BlockSpec((1, HEAD_DIM), const), pl.BlockSpec((1, HEAD_DIM), const), pl.BlockSpec((1, HEAD_DIM), const),
            pl.BlockSpec((len(POOL_WINDOWS), POOL_GROUP_DIM, POOL_GROUP_DIM), lambda i: (0, 0, 0)),
            pl.BlockSpec((1, D_POOL), const),
        ],
        out_specs=[
            pl.BlockSpec((tm, D_POOL), row), head_major, head_major, head_major,
            pl.BlockSpec((tm, D_MEM), row), pl.BlockSpec((tm, D_POOL), row),
        ],
        out_shape=[
            jax.ShapeDtypeStruct((n, D_POOL), F32), hm_shape, hm_shape, hm_shape,
            jax.ShapeDtypeStruct((n, D_MEM), BF16), jax.ShapeDtypeStruct((n, D_POOL), BF16),
        ],
        scratch_shapes=[pltpu.VMEM((tm + POOL_HALO, D_POOL), F32)],
        compiler_params=_params("arbitrary"),
        name="inproj_prompt",
    )(x2d, g1, w_in_b, *tables, qg, kg, mqg, pool_w_b, pool_scale)


def _inproj_sample_kernel(x_ref, g1_ref, w_ref, cos_ref, slo_ref, shi_ref, qg_ref, kg_ref, mqg_ref,
                          pw_ref, pscale_ref, state_ref,
                          u_ref, q_ref, k_ref, v_ref, qm_ref, yp_ref, *, past_len):
    hb = _rms(x_ref[...], g1_ref[...]).astype(BF16)
    u = _dot(hb, w_ref[:, 0:D_POOL])
    u_ref[...] = u
    db = state_ref.shape[0]
    srow = lax.broadcasted_iota(jnp.int32, (db, POOL_STATE, POOL_GROUP_DIM), 1)

    def win_sum(g, w):
        sl = slice(g * POOL_GROUP_DIM, (g + 1) * POOL_GROUP_DIM)
        hist = jnp.where(srow >= POOL_STATE - (w - 1), state_ref[:, :, sl], 0.0)
        return u[:, sl] + jnp.sum(hist, axis=1)

    _pool_branch(u, win_sum, lambda w: float(min(w, past_len + 1)), pw_ref, pscale_ref, yp_ref)

    cos, slo, shi = cos_ref[...], slo_ref[...], shi_ref[...]
    c0 = D_POOL
    qf = _dot(hb, w_ref[:, c0:c0 + D_ATTN])
    c0 += D_ATTN
    kf = _dot(hb, w_ref[:, c0:c0 + D_ATTN])
    for h in range(N_HEADS):
        sl = slice(h * HEAD_DIM, (h + 1) * HEAD_DIM)
        q_ref[:, sl] = _rope(_rms(qf[:, sl], qg_ref[...]), cos, slo, shi)
        k_ref[:, sl] = _rope(_rms(kf[:, sl], kg_ref[...]), cos, slo, shi)
    c0 += D_ATTN
    v_ref[...] = _dot(hb, w_ref[:, c0:c0 + D_ATTN])
    c0 += D_ATTN
    mf = _dot(hb, w_ref[:, c0:c0 + D_MEM])
    for h in range(MEM_HEADS):
        sl = slice(h * HEAD_DIM, (h + 1) * HEAD_DIM)
        qm_ref[:, sl] = _rms(mf[:, sl], mqg_ref[...]).astype(qm_ref.dtype)


def _inproj_sample(x2d, g1, w_in_b, tables, qg, kg, mqg, pool_w_b, pool_scale, state_pool, *, past_len):
    db = x2d.shape[0]
    full = lambda shape: pl.BlockSpec(shape, lambda i: (0,) * len(shape))
    return pl.pallas_call(
        functools.partial(_inproj_sample_kernel, past_len=past_len),
        grid=(1,),
        in_specs=[
            full((db, D_MODEL)), full((1, D_MODEL)),
            pl.BlockSpec((D_MODEL, D_QKV), lambda i: (0, 0), pipeline_mode=pl.Buffered(1)),
            full((db, HEAD_DIM)), full((db, HEAD_DIM)), full((db, HEAD_DIM)),
            full((1, HEAD_DIM)), full((1, HEAD_DIM)), full((1, HEAD_DIM)),
            full((len(POOL_WINDOWS), POOL_GROUP_DIM, POOL_GROUP_DIM)), full((1, D_POOL)),
            full((db, POOL_STATE, D_POOL)),
        ],
        out_specs=[full((db, D_POOL)), full((db, D_ATTN)), full((db, D_ATTN)), full((db, D_ATTN)),
                   full((db, D_MEM)), full((db, D_POOL))],
        out_shape=[jax.ShapeDtypeStruct((db, D_POOL), F32), jax.ShapeDtypeStruct((db, D_ATTN), F32),
                   jax.ShapeDtypeStruct((db, D_ATTN), F32), jax.ShapeDtypeStruct((db, D_ATTN), F32),
                   jax.ShapeDtypeStruct((db, D_MEM), BF16), jax.ShapeDtypeStruct((db, D_POOL), BF16)],
        compiler_params=_params("arbitrary"),
        name="inproj_sample",
    )(x2d, g1, w_in_b, *tables, qg, kg, mqg, pool_w_b, pool_scale, state_pool)


def _moba_prompt_kernel(q_ref, k_ref, v_ref, o_ref, kb_ref, vb_ref, km_ref, m_ref, l_ref, acc_ref, *, seq):
    nb = seq // MOBA_BLOCK
    k = k_ref[0, 0]
    kb_ref[...] = k.astype(BF16)
    vb_ref[...] = v_ref[0, 0].astype(BF16)
    km_ref[...] = jnp.zeros(km_ref.shape, F32)
    km_ref[0:nb, :] = jnp.sum(k.reshape(nb, MOBA_BLOCK, HEAD_DIM), axis=1) * (1.0 / MOBA_BLOCK)
    kmean = km_ref[...]

    tri = (lax.broadcasted_iota(jnp.int32, (MOBA_BLOCK, MOBA_BLOCK), 1)
           <= lax.broadcasted_iota(jnp.int32, (MOBA_BLOCK, MOBA_BLOCK), 0))
    lane = lax.broadcasted_iota(jnp.int32, (MOBA_BLOCK, LANES), 1)

    def q_block(bq, carry):
        r0 = pl.multiple_of(bq * MOBA_BLOCK, MOBA_BLOCK)
        qf = q_ref[0, 0, pl.ds(r0, MOBA_BLOCK), :]
        gs = lax.dot_general(qf, kmean, (((1,), (1,)), ((), ())), preferred_element_type=F32,
                             precision=lax.Precision.HIGHEST)
        gs = jnp.where(lane < bq, gs, NEG)
        i0, i1, i2 = _top3(gs, bq)
        qs = (qf * SCALE).astype(BF16)

        s = jnp.where(tri, _dot_t(qs, kb_ref[pl.ds(r0, MOBA_BLOCK), :]), NEG)
        m = jnp.max(s, axis=-1, keepdims=True)
        p = jnp.exp(s - m)
        m_ref[...] = jnp.broadcast_to(m, m_ref.shape)
        l_ref[...] = jnp.broadcast_to(jnp.sum(p, axis=-1, keepdims=True), l_ref.shape)
        acc_ref[...] = _dot(p.astype(BF16), vb_ref[pl.ds(r0, MOBA_BLOCK), :])

        def kv_block(j, c):
            c0 = pl.multiple_of(j * MOBA_BLOCK, MOBA_BLOCK)
            bias = jnp.where((i0 == j) | (i1 == j) | (i2 == j), 0.0, NEG)
            s = _dot_t(qs, kb_ref[pl.ds(c0, MOBA_BLOCK), :]) + bias
            m_prev = m_ref[...]
            m_next = jnp.maximum(m_prev, jnp.max(s, axis=-1, keepdims=True))
            alpha = jnp.exp(m_prev - m_next)
            p = jnp.exp(s - jnp.concatenate([m_next] * (MOBA_BLOCK // LANES), axis=1))
            l_ref[...] = alpha * l_ref[...] + jnp.sum(p, axis=-1, keepdims=True)
            acc_ref[...] = alpha * acc_ref[...] + _dot(p.astype(BF16), vb_ref[pl.ds(c0, MOBA_BLOCK), :])
            m_ref[...] = m_next
            return c

        lax.fori_loop(0, bq, kv_block, 0)
        o_ref[pl.ds(r0, MOBA_BLOCK), :] = (acc_ref[...] / l_ref[...]).astype(o_ref.dtype)
        return carry

    lax.fori_loop(0, nb, q_block, 0)


def _moba_prompt(q, k, v):
    b, h, s, _ = q.shape
    blk = pl.BlockSpec((1, 1, s, HEAD_DIM), lambda bi, hi: (bi, hi, 0, 0))
    return pl.pallas_call(
        functools.partial(_moba_prompt_kernel, seq=s),
        grid=(b, h),
        in_specs=[blk, blk, blk],
        out_specs=pl.BlockSpec((s, HEAD_DIM), lambda bi, hi: (bi, hi)),
        out_shape=jax.ShapeDtypeStruct((b * s, h * HEAD_DIM), BF16),
        scratch_shapes=[pltpu.VMEM((s, HEAD_DIM), BF16), pltpu.VMEM((s, HEAD_DIM), BF16),
                        pltpu.VMEM((LANES, HEAD_DIM), F32),
                        pltpu.VMEM((MOBA_BLOCK, LANES), F32), pltpu.VMEM((MOBA_BLOCK, LANES), F32),
                        pltpu.VMEM((MOBA_BLOCK, HEAD_DIM), F32)],
        compiler_params=_params("arbitrary", "arbitrary"),
        name="moba_prompt",
    )(q, k, v)


def _mem_attend_kernel(q_ref, k_ref, v_ref, o_ref):
    for h in range(MEM_HEADS):
        sl = slice(h * HEAD_DIM, (h + 1) * HEAD_DIM)
        qs = (q_ref[:, sl].astype(F32) * SCALE).astype(BF16)
        s = _dot_t(qs, k_ref[0][:, sl].astype(BF16))
        m = jnp.max(s, axis=-1, keepdims=True)
        p = jnp.exp(s - m)
        l = jnp.sum(p, axis=-1, keepdims=True)
        o = _dot(p.astype(BF16), v_ref[0][:, sl].astype(BF16)) / l
        o_ref[:, sl] = o.astype(o_ref.dtype)


def _mem_attend(qm, mem_k, mem_v, *, rows_per_seq, tq):
    n = qm.shape[0]
    m_tok = mem_k.shape[1]
    tiles = rows_per_seq // tq
    kv = pl.BlockSpec((1, m_tok, D_MEM), lambda i: (i // tiles, 0, 0))
    return pl.pallas_call(
        _mem_attend_kernel,
        grid=(n // tq,),
        in_specs=[pl.BlockSpec((tq, D_MEM), lambda i: (i, 0)), kv, kv],
        out_specs=pl.BlockSpec((tq, D_MEM), lambda i: (i, 0)),
        out_shape=jax.ShapeDtypeStruct((n, D_MEM), BF16),
        compiler_params=_params("arbitrary"),
        name="mem_attend",
    )(qm, mem_k, mem_v)


def _mem_kv_kernel(x_ref, w_ref, g_ref, k_ref, v_ref):
    kv = _dot(x_ref[...].astype(BF16), w_ref[...])
    for h in range(MEM_HEADS):
        sl = slice(h * HEAD_DIM, (h + 1) * HEAD_DIM)
        k_ref[:, sl] = _rms(kv[:, sl], g_ref[...])
    v_ref[...] = kv[:, D_MEM:]


def _mem_kv(mem2d, w_b, g, *, tm):
    n = mem2d.shape[0]
    out = jax.ShapeDtypeStruct((n, D_MEM), F32)
    return pl.pallas_call(
        _mem_kv_kernel,
        grid=(n // tm,),
        in_specs=[pl.BlockSpec((tm, D_MODEL), lambda i: (i, 0)),
                  pl.BlockSpec((D_MODEL, 2 * D_MEM), lambda i: (0, 0)),
                  pl.BlockSpec((1, HEAD_DIM), lambda i: (0, 0))],
        out_specs=[pl.BlockSpec((tm, D_MEM), lambda i: (i, 0))] * 2,
        out_shape=[out, out],
        compiler_params=_params("arbitrary"),
        name="mem_kv",
    )(mem2d, w_b, g)


def _merge_kernel(x_ref, g1_ref, yp_ref, ya_ref, ym_ref, wg0_ref, wg1_ref, wg2_ref,
                  wbp_ref, wba_ref, wbm_ref, wo_ref, o_ref, hb_ref, acc_ref):
    c = pl.program_id(1)

    @pl.when(c == 0)
    def _():
        hb_ref[...] = _rms(x_ref[...], g1_ref[...]).astype(BF16)
        acc_ref[...] = jnp.zeros(acc_ref.shape, F32)

    hb = hb_ref[...]
    merged = (jax.nn.sigmoid(_dot(hb, wg0_ref[...])) * _dot(yp_ref[...].astype(BF16), wbp_ref[...])
              + jax.nn.sigmoid(_dot(hb, wg1_ref[...])) * _dot(ya_ref[...].astype(BF16), wba_ref[...])
              + jax.nn.sigmoid(_dot(hb, wg2_ref[...])) * _dot(ym_ref[...].astype(BF16), wbm_ref[...]))
    acc_ref[...] += _dot(merged.astype(BF16), wo_ref[...])

    @pl.when(c == pl.num_programs(1) - 1)
    def _():
        o_ref[...] = x_ref[...] + acc_ref[...]


def _merge(x2d, g1, yp, ya, ym, w_in_b, wbp_b, wba_b, wbm_b, wo_b, *, tm):
    n = x2d.shape[0]
    tc = MERGE_CHUNK
    n_chunks = D_MODEL // tc
    gate0 = D_QKV // tc
    row = lambda i, c: (i, 0)

    def gate_spec(branch):
        return pl.BlockSpec((D_MODEL, tc), lambda i, c: (0, gate0 + branch * n_chunks + c))

    col = lambda rows: pl.BlockSpec((rows, tc), lambda i, c: (0, c))
    return pl.pallas_call(
        _merge_kernel,
        grid=(n // tm, n_chunks),
        in_specs=[pl.BlockSpec((tm, D_MODEL), row), pl.BlockSpec((1, D_MODEL), lambda i, c: (0, 0)),
                  pl.BlockSpec((tm, D_POOL), row), pl.BlockSpec((tm, D_ATTN), row), pl.BlockSpec((tm, D_MEM), row),
                  gate_spec(0), gate_spec(1), gate_spec(2),
                  col(D_POOL), col(D_ATTN), col(D_MEM),
                  pl.BlockSpec((tc, D_MODEL), lambda i, c: (c, 0))],
        out_specs=pl.BlockSpec((tm, D_MODEL), row),
        out_shape=jax.ShapeDtypeStruct((n, D_MODEL), F32),
        scratch_shapes=[pltpu.VMEM((tm, D_MODEL), BF16), pltpu.VMEM((tm, D_MODEL), F32)],
        compiler_params=_params("arbitrary", "arbitrary"),
        name="merge",
    )(x2d, g1, yp, ya, ym, w_in_b, w_in_b, w_in_b, wbp_b, wba_b, wbm_b, wo_b)


def _ffn_tail(c, x_ref, ua, ug, a_m1, a_m2, g_m1, g_m2, cwa_ref, cwg_ref, cba_ref, cbg_ref, wd_ref, o_ref, acc_ref):
    ca = cba_ref[...] + cwa_ref[0:1, :] * a_m2 + cwa_ref[1:2, :] * a_m1 + cwa_ref[2:3, :] * ua
    cg = cbg_ref[...] + cwg_ref[0:1, :] * g_m2 + cwg_ref[1:2, :] * g_m1 + cwg_ref[2:3, :] * ug
    act = (jax.nn.silu(ca) * cg).astype(BF16)
    acc_ref[...] += _dot(act, wd_ref[...])

    @pl.when(c == pl.num_programs(1) - 1)
    def _():
        o_ref[...] = x_ref[...] + acc_ref[...]


def _ffn_head(c, x_ref, g2_ref, hb_ref, acc_ref):
    @pl.when(c == 0)
    def _():
        hb_ref[...] = _rms(x_ref[...], g2_ref[...]).astype(BF16)
        acc_ref[...] = jnp.zeros(acc_ref.shape, F32)


def _ffn_seq_kernel(x_ref, g2_ref, wua_ref, wug_ref, cwa_ref, cwg_ref, cba_ref, cbg_ref, wd_ref,
                    o_ref, sta_ref, stg_ref,
                    hb_ref, acc_ref, exta_ref, extg_ref, carry_ref, *, tm, tiles_per_seq):
    i, c = pl.program_id(0), pl.program_id(1)
    _ffn_head(c, x_ref, g2_ref, hb_ref, acc_ref)
    hb = hb_ref[...]
    ua = _dot(hb, wua_ref[...])
    ug = _dot(hb, wug_ref[...])
    n_prev = CONV_WIDTH - 1
    lo = CONV_HALO - n_prev

    @pl.when(i % tiles_per_seq == 0)
    def _():
        carry_ref[c] = jnp.zeros(carry_ref.shape[1:], F32)

    exta_ref[lo:CONV_HALO, :] = carry_ref[c, 0:n_prev, :]
    extg_ref[lo:CONV_HALO, :] = carry_ref[c, n_prev:2 * n_prev, :]
    exta_ref[CONV_HALO:CONV_HALO + tm, :] = ua
    extg_ref[CONV_HALO:CONV_HALO + tm, :] = ug
    a_last = ua[tm - n_prev:tm, :]
    g_last = ug[tm - n_prev:tm, :]
    carry_ref[c, 0:n_prev, :] = a_last
    carry_ref[c, n_prev:2 * n_prev, :] = g_last

    @pl.when(i % tiles_per_seq == tiles_per_seq - 1)
    def _():
        sta_ref[i // tiles_per_seq, c] = a_last
        stg_ref[i // tiles_per_seq, c] = g_last
    _ffn_tail(c, x_ref, ua, ug,
              exta_ref[CONV_HALO - 1:CONV_HALO - 1 + tm, :], exta_ref[CONV_HALO - 2:CONV_HALO - 2 + tm, :],
              extg_ref[CONV_HALO - 1:CONV_HALO - 1 + tm, :], extg_ref[CONV_HALO - 2:CONV_HALO - 2 + tm, :],
              cwa_ref, cwg_ref, cba_ref, cbg_ref, wd_ref, o_ref, acc_ref)


def _ffn_step_kernel(x_ref, g2_ref, wua_ref, wug_ref, cwa_ref, cwg_ref, cba_ref, cbg_ref, wd_ref,
                     am2_ref, gm2_ref, am1_ref, gm1_ref,
                     o_ref, ua_ref, ug_ref, hb_ref, acc_ref):
    c = pl.program_id(1)
    _ffn_head(c, x_ref, g2_ref, hb_ref, acc_ref)
    hb = hb_ref[...]
    ua = _dot(hb, wua_ref[...])
    ug = _dot(hb, wug_ref[...])
    ua_ref[...] = ua
    ug_ref[...] = ug
    _ffn_tail(c, x_ref, ua, ug, am1_ref[...], am2_ref[...], gm1_ref[...], gm2_ref[...],
              cwa_ref, cwg_ref, cba_ref, cbg_ref, wd_ref, o_ref, acc_ref)


def _ffn(x2d, g2, w_up_b, conv_w, conv_b2, w_down_b, *, tm, batch=None, seq=None, prev=None):
    n = x2d.shape[0]
    tc = FFN_CHUNK
    n_chunks = D_FF // tc
    row = lambda i, c: (i, 0)
    a_col = lambda rows: pl.BlockSpec((rows, tc), lambda i, c: (0, c))
    g_col = lambda rows: pl.BlockSpec((rows, tc), lambda i, c: (0, n_chunks + c))
    common_in = [pl.BlockSpec((tm, D_MODEL), row), pl.BlockSpec((1, D_MODEL), lambda i, c: (0, 0)),
                 a_col(D_MODEL), g_col(D_MODEL), a_col(CONV_WIDTH), g_col(CONV_WIDTH), a_col(1), g_col(1),
                 pl.BlockSpec((tc, D_MODEL), lambda i, c: (c, 0))]
    common_args = (x2d, g2, w_up_b, w_up_b, conv_w, conv_w, conv_b2, conv_b2, w_down_b)
    common_scratch = [pltpu.VMEM((tm, D_MODEL), BF16), pltpu.VMEM((tm, D_MODEL), F32)]
    y_spec = pl.BlockSpec((tm, D_MODEL), row)
    y_shape = jax.ShapeDtypeStruct((n, D_MODEL), F32)
    if prev is None:
        tiles_per_seq = seq // tm
        n_prev = CONV_WIDTH - 1
        st_spec = pl.BlockSpec((batch, n_chunks, n_prev, tc), lambda i, c: (0, 0, 0, 0))
        st_shape = jax.ShapeDtypeStruct((batch, n_chunks, n_prev, tc), F32)
        y, sta, stg = pl.pallas_call(
            functools.partial(_ffn_seq_kernel, tm=tm, tiles_per_seq=tiles_per_seq),
            grid=(n // tm, n_chunks),
            in_specs=common_in,
            out_specs=[y_spec, st_spec, st_spec],
            out_shape=[y_shape, st_shape, st_shape],
            scratch_shapes=common_scratch + [pltpu.VMEM((tm + CONV_HALO, tc), F32), pltpu.VMEM((tm + CONV_HALO, tc), F32),
                                             pltpu.VMEM((n_chunks, SUBLANES, tc), F32)],
            compiler_params=_params("arbitrary", "arbitrary"),
            name="ffn_seq",
        )(*common_args)
        unchunk = lambda st: st.transpose(0, 2, 1, 3).reshape(batch, n_prev, D_FF)
        return y, unchunk(sta), unchunk(stg)
    prev2, prev1 = prev
    rows_a = pl.BlockSpec((tm, tc), lambda i, c: (i, c))
    rows_g = pl.BlockSpec((tm, tc), lambda i, c: (i, n_chunks + c))
    up_shape = jax.ShapeDtypeStruct((n, D_FF), F32)
    return pl.pallas_call(
        _ffn_step_kernel,
        grid=(n // tm, n_chunks),
        in_specs=common_in + [rows_a, rows_g, rows_a, rows_g],
        out_specs=[y_spec, rows_a, rows_a],
        out_shape=[y_shape, up_shape, up_shape],
        scratch_shapes=common_scratch,
        compiler_params=_params("arbitrary", "arbitrary"),
        name="ffn_step",
    )(*common_args, prev2, prev2, prev1, prev1)


def _page_copy(pt_ref, cache_ref, buf_ref, sem_ref, b, p, slot):
    return pltpu.make_async_copy(cache_ref.at[0, pt_ref[b, p]], buf_ref.at[slot], sem_ref.at[slot])


def _page_sums_kernel(pt_ref, cache_ref, o_ref, buf_ref, sem_ref, *, n_pages):
    b = pl.program_id(0)
    for s in range(PAGE_RING):
        _page_copy(pt_ref, cache_ref, buf_ref, sem_ref, b, s, s).start()

    def body(p, carry):
        slot = p % PAGE_RING
        _page_copy(pt_ref, cache_ref, buf_ref, sem_ref, b, p, slot).wait()
        sums = jnp.sum(buf_ref[slot], axis=1)
        for h in range(N_HEADS):
            o_ref[0, h, pl.ds(p, 1), :] = sums[h:h + 1, :]

        @pl.when(p + PAGE_RING < n_pages)
        def _():
            _page_copy(pt_ref, cache_ref, buf_ref, sem_ref, b, p + PAGE_RING, slot).start()

        return carry

    lax.fori_loop(0, n_pages, body, 0)


def _page_sums(page_table, cache_k):
    db, n_pages = page_table.shape
    assert n_pages >= PAGE_RING
    return pl.pallas_call(
        functools.partial(_page_sums_kernel, n_pages=n_pages),
        grid_spec=pltpu.PrefetchScalarGridSpec(
            num_scalar_prefetch=1,
            grid=(db,),
            in_specs=[pl.BlockSpec(memory_space=pl.ANY)],
            out_specs=pl.BlockSpec((1, N_HEADS, n_pages, HEAD_DIM), lambda b, pt: (b, 0, 0, 0)),
            scratch_shapes=[pltpu.VMEM((PAGE_RING, N_HEADS, PAGE_SIZE, HEAD_DIM), F32),
                            pltpu.SemaphoreType.DMA((PAGE_RING,))],
        ),
        out_shape=jax.ShapeDtypeStruct((db, N_HEADS, n_pages, HEAD_DIM), F32),
        compiler_params=_params("arbitrary"),
        name="page_sums",
    )(page_table, cache_k)


def _block_gate_kernel(q_ref, ps_ref, o_ref, km_ref, *, n_blocks):
    km_ref[...] = jnp.zeros(km_ref.shape, F32)
    lane = lax.broadcasted_iota(jnp.int32, (SUBLANES, LANES), 1)
    for h in range(N_HEADS):
        even = ps_ref[0, h, pl.ds(0, n_blocks, stride=PAGES_PER_BLOCK), :]
        odd = ps_ref[0, h, pl.ds(1, n_blocks, stride=PAGES_PER_BLOCK), :]
        km_ref[0:n_blocks, :] = (even + odd) * (1.0 / MOBA_BLOCK)
        qh = jnp.broadcast_to(q_ref[0, :, h * HEAD_DIM:(h + 1) * HEAD_DIM], (SUBLANES, HEAD_DIM))
        gs = lax.dot_general(qh, km_ref[...], (((1,), (1,)), ((), ())), preferred_element_type=F32,
                             precision=lax.Precision.HIGHEST)
        gs = jnp.where(lane < n_blocks, gs, NEG)
        i0, i1, i2 = _top3(gs, n_blocks)
        picked = jnp.where(lane == 0, i0, jnp.where(lane == 1, i1, jnp.where(lane == 2, i2, 0)))
        o_ref[0, h:h + 1, :] = picked[0:1, :]


def _block_gate(q3, page_sums):
    db, _, n_pages, _ = page_sums.shape
    n_blocks = n_pages // PAGES_PER_BLOCK
    assert PAGES_PER_BLOCK == 2 and MOBA_TOPK <= n_blocks <= LANES
    return pl.pallas_call(
        functools.partial(_block_gate_kernel, n_blocks=n_blocks),
        grid=(db,),
        in_specs=[pl.BlockSpec((1, 1, D_ATTN), lambda b: (b, 0, 0)),
                  pl.BlockSpec((1, N_HEADS, n_pages, HEAD_DIM), lambda b: (b, 0, 0, 0))],
        out_specs=pl.BlockSpec((1, N_HEADS, LANES), lambda b: (b, 0, 0)),
        out_shape=jax.ShapeDtypeStruct((db, N_HEADS, LANES), jnp.int32),
        scratch_shapes=[pltpu.VMEM((LANES, HEAD_DIM), F32)],
        compiler_params=_params("arbitrary"),
        name="block_gate",
    )(q3, page_sums)


N_SEL_PAGES = MOBA_TOPK * PAGES_PER_BLOCK


def _sel_copy(pt_ref, sel_ref, cache_ref, buf_ref, sem_ref, b, h, s):
    blk = sel_ref[b, h * MOBA_TOPK + s // PAGES_PER_BLOCK]
    page = pt_ref[b, blk * PAGES_PER_BLOCK + s % PAGES_PER_BLOCK]
    return pltpu.make_async_copy(cache_ref.at[0, page, h], buf_ref.at[h, s], sem_ref.at[h])


def _moba_sample_kernel(pt_ref, sel_ref, q_ref, kn_ref, vn_ref, ck_ref, cv_ref, o_ref,
                        kbuf_ref, vbuf_ref, ksem_ref, vsem_ref):
    b = pl.program_id(0)
    for h in range(N_HEADS):
        for s in range(N_SEL_PAGES):
            _sel_copy(pt_ref, sel_ref, ck_ref, kbuf_ref, ksem_ref, b, h, s).start()
            _sel_copy(pt_ref, sel_ref, cv_ref, vbuf_ref, vsem_ref, b, h, s).start()
    for h in range(N_HEADS):
        for s in range(N_SEL_PAGES):
            _sel_copy(pt_ref, sel_ref, ck_ref, kbuf_ref, ksem_ref, b, h, s).wait()
            _sel_copy(pt_ref, sel_ref, cv_ref, vbuf_ref, vsem_ref, b, h, s).wait()
        sl = slice(h * HEAD_DIM, (h + 1) * HEAD_DIM)
        qs = q_ref[0, :, sl] * SCALE
        q8 = jnp.broadcast_to(qs, (SUBLANES, HEAD_DIM)).astype(BF16)
        kk = kbuf_ref[h].reshape(N_SEL_PAGES * PAGE_SIZE, HEAD_DIM).astype(BF16)
        vv = vbuf_ref[h].reshape(N_SEL_PAGES * PAGE_SIZE, HEAD_DIM).astype(BF16)
        s_sel = _dot_t(q8, kk)
        s_own = jnp.sum(qs * kn_ref[0, :, sl], axis=-1, keepdims=True)
        m = jnp.maximum(jnp.max(s_sel, axis=-1, keepdims=True), s_own)
        p = jnp.exp(s_sel - m)
        p_own = jnp.exp(s_own - m)
        l = jnp.sum(p, axis=-1, keepdims=True) + p_own
        o = (_dot(p.astype(BF16), vv) + p_own * vn_ref[0, :, sl]) / l
        o_ref[0, :, sl] = o[0:1, :].astype(o_ref.dtype)


def _moba_sample(page_table, sel, q3, k3, v3, cache_k, cache_v):
    db = q3.shape[0]
    tok = pl.BlockSpec((1, 1, D_ATTN), lambda b, pt, sl: (b, 0, 0))
    return pl.pallas_call(
        _moba_sample_kernel,
        grid_spec=pltpu.PrefetchScalarGridSpec(
            num_scalar_prefetch=2,
            grid=(db,),
            in_specs=[tok, tok, tok, pl.BlockSpec(memory_space=pl.ANY), pl.BlockSpec(memory_space=pl.ANY)],
            out_specs=tok,
            scratch_shapes=[pltpu.VMEM((N_HEADS, N_SEL_PAGES, PAGE_SIZE, HEAD_DIM), F32),
                            pltpu.VMEM((N_HEADS, N_SEL_PAGES, PAGE_SIZE, HEAD_DIM), F32),
                            pltpu.SemaphoreType.DMA((N_HEADS,)), pltpu.SemaphoreType.DMA((N_HEADS,))],
        ),
        out_shape=jax.ShapeDtypeStruct((db, 1, D_ATTN), F32),
        compiler_params=_params("arbitrary"),
        name="moba_sample",
    )(page_table, sel, q3, k3, v3, cache_k, cache_v)


def _row_tile(n, pref):
    return pref if n % pref == 0 else n


def kernel(x_prompt, x_sample, mem_prompt, cache_k, cache_v, page_table, cache_mem_k, cache_mem_v, state_pool, state_conv, norm1_g, w_in, q_norm_g, k_norm_g, mem_q_norm_g, mem_k_norm_g, w_mem_kv, pool_w, pool_scale, w_branch_pool, w_branch_attn, w_branch_mem, w_out, norm2_g, w_up, conv_w, conv_b, w_down):
    b, s, _ = x_prompt.shape
    db, t, _ = x_sample.shape
    assert t == 1 and norm1_g.shape[0] == 1 and s % MOBA_BLOCK == 0
    n_pages = page_table.shape[1]
    assert n_pages % PAGES_PER_BLOCK == 0
    m_tok = mem_prompt.shape[1]

    w_in_b = w_in[0].astype(BF16)
    wbp_b, wba_b, wbm_b = (w[0].astype(BF16) for w in (w_branch_pool, w_branch_attn, w_branch_mem))
    wo_b, w_up_b, w_down_b = w_out[0].astype(BF16), w_up[0].astype(BF16), w_down[0].astype(BF16)
    w_mem_b, pool_w_b = w_mem_kv[0].astype(BF16), pool_w[0].astype(BF16)
    g1, g2 = norm1_g, norm2_g
    conv_b2 = conv_b
    norm_args = (q_norm_g, k_norm_g, mem_q_norm_g, pool_w_b, pool_scale)

    n = b * s
    x2d = x_prompt.reshape(n, D_MODEL)
    tm = _row_tile(s, 512)
    tables_p = _rope_tables(jnp.arange(s, dtype=jnp.int32))
    u_p, q_p, k_p, v_p, qm_p, yp_p = _inproj_prompt(x2d, g1, w_in_b, tables_p, *norm_args, batch=b, seq=s, tm=tm)
    mk2d, mv2d = _mem_kv(mem_prompt.reshape(b * m_tok, D_MODEL), w_mem_b, mem_k_norm_g, tm=m_tok)
    ya_p = _moba_prompt(q_p, k_p, v_p)
    ym_p = _mem_attend(qm_p, mk2d.reshape(b, m_tok, D_MEM), mv2d.reshape(b, m_tok, D_MEM),
                       rows_per_seq=s, tq=_row_tile(s, 1024))
    x1_p = _merge(x2d, g1, yp_p, ya_p, ym_p, w_in_b, wbp_b, wba_b, wbm_b, wo_b, tm=tm)
    y_p, sta_p, stg_p = _ffn(x1_p, g2, w_up_b, conv_w[0], conv_b2, w_down_b, tm=tm, batch=b, seq=s)

    xs2d = x_sample.reshape(db, D_MODEL)
    past_len = n_pages * PAGE_SIZE
    pos_s = jnp.full((db,), past_len, jnp.int32)
    u_s, q_s, k_s, v_s, qm_s, yp_s = _inproj_sample(xs2d, g1, w_in_b, _rope_tables(pos_s), *norm_args, state_pool[0],
                                                    past_len=past_len)
    q3, k3, v3 = (a.reshape(db, 1, D_ATTN) for a in (q_s, k_s, v_s))
    sums = _page_sums(page_table, cache_k)
    sel = _block_gate(q3, sums)[:, :, :MOBA_TOPK].reshape(db, N_HEADS * MOBA_TOPK)
    ya_s = _moba_sample(page_table, sel, q3, k3, v3, cache_k, cache_v).reshape(db, D_ATTN)
    qm_rep = jnp.broadcast_to(qm_s[:, None, :], (db, BF16_ROWS, D_MEM)).reshape(db * BF16_ROWS, D_MEM)
    ym_s = _mem_attend(qm_rep, cache_mem_k[0].reshape(db, m_tok, D_MEM), cache_mem_v[0].reshape(db, m_tok, D_MEM),
                       rows_per_seq=BF16_ROWS, tq=BF16_ROWS).reshape(db, BF16_ROWS, D_MEM)[:, 0]
    x1_s = _merge(xs2d, g1, yp_s, ya_s, ym_s, w_in_b, wbp_b, wba_b, wbm_b, wo_b, tm=db)
    y_s, ua_s, ug_s = _ffn(x1_s, g2, w_up_b, conv_w[0], conv_b2, w_down_b, tm=db,
                           prev=(state_conv[0, :, 0], state_conv[0, :, 1]))

    up_s = jnp.concatenate([ua_s, ug_s], axis=-1)
    return (
        y_p.reshape(b, s, D_MODEL),
        y_s.reshape(db, 1, D_MODEL),
        k_p[None], v_p[None],
        k_s.reshape(1, db, N_HEADS, 1, HEAD_DIM), v_s.reshape(1, db, N_HEADS, 1, HEAD_DIM),
        mk2d.reshape(1, b, m_tok, MEM_HEADS, HEAD_DIM), mv2d.reshape(1, b, m_tok, MEM_HEADS, HEAD_DIM),
        u_p.reshape(b, s, D_POOL)[None, :, s - POOL_STATE:],
        jnp.concatenate([state_pool[0][:, 1:], u_s[:, None]], axis=1)[None],
        jnp.concatenate([sta_p, stg_p], axis=-1)[None],
        jnp.concatenate([state_conv[0][:, 1:], up_s[:, None]], axis=1)[None],
    )
```

```python
import functools

import jax
import jax.numpy as jnp
import numpy as np
from jax import lax
from jax.experimental import pallas as pl
from jax.experimental.pallas import tpu as pltpu

F32 = jnp.float32
BF16 = jnp.bfloat16

D_MODEL = 2048
HEAD_DIM = 128
N_HEADS = 8
MEM_HEADS = 4
POOL_WINDOWS = (2, 4, 8, 16)
POOL_GROUP_DIM = 128
D_POOL = len(POOL_WINDOWS) * POOL_GROUP_DIM
POOL_STATE = max(POOL_WINDOWS) - 1
D_ATTN = N_HEADS * HEAD_DIM
D_MEM = MEM_HEADS * HEAD_DIM
D_QKV = D_POOL + 3 * D_ATTN + D_MEM
N_BRANCH = 3
MOBA_BLOCK = 256
MOBA_TOPK = 3
PAGE_SIZE = 128
PAGES_PER_BLOCK = MOBA_BLOCK // PAGE_SIZE
ROPE_THETA = 500000.0
ROPE_DIM = HEAD_DIM // 4
ROPE_HALF = ROPE_DIM // 2
D_FF = 5632
CONV_WIDTH = 3
EPS = 1e-6
NEG = -1e30
SCALE = HEAD_DIM ** -0.5
LOG2E = 1.4426950408889634

LANES = 128
SUBLANES = 8
BF16_ROWS = 2 * SUBLANES
POOL_HALO = 16
CONV_HALO = SUBLANES
VMEM_LIMIT = 56 * 1024 * 1024
FFN_CHUNK = 512
MERGE_CHUNK = 512
PAGE_RING = 16


def _params(*sem):
    return pltpu.CompilerParams(dimension_semantics=sem, vmem_limit_bytes=VMEM_LIMIT)


def _rms(x, g):
    ms = jnp.mean(x * x, axis=-1, keepdims=True)
    return x * lax.rsqrt(ms + EPS) * g


def _rope(t, cos, sin_lo, sin_hi):
    return (t * cos + pltpu.roll(t, ROPE_HALF, 1) * sin_hi
            + pltpu.roll(t, HEAD_DIM - ROPE_HALF, 1) * sin_lo)


def _rope_tables(pos):
    half = ROPE_HALF
    inv = ROPE_THETA ** (-jnp.arange(half, dtype=F32) * (2.0 / ROPE_DIM))
    ang = pos.astype(F32)[:, None] * inv[None, :]
    cos, sin = jnp.cos(ang), jnp.sin(ang)
    n = pos.shape[0]
    pad = jnp.zeros((n, HEAD_DIM - ROPE_DIM), F32)
    zero = jnp.zeros((n, half), F32)
    cos_t = jnp.concatenate([cos, cos, pad + 1.0], axis=1)
    sin_lo = jnp.concatenate([-sin, zero, pad], axis=1)
    sin_hi = jnp.concatenate([zero, sin, pad], axis=1)
    return cos_t, sin_lo, sin_hi


def _dot(a, b):
    return jnp.dot(a, b, preferred_element_type=F32)


def _dot_t(a, b):
    return lax.dot_general(a, b, (((1,), (1,)), ((), ())), preferred_element_type=F32)


def _top3(gs, n_valid):
    lane = lax.broadcasted_iota(jnp.int32, gs.shape, 1)
    out = []
    for r in range(MOBA_TOPK):
        mx = jnp.max(gs, axis=-1, keepdims=True)
        idx = jnp.min(jnp.where(gs == mx, lane, LANES), axis=-1, keepdims=True)
        out.append(jnp.where(r < n_valid, idx, -1))
        gs = jnp.where(lane == idx, -jnp.inf, gs)
    return out


def _pool_branch(u, win_sum, cnt, pw_ref, pscale_ref, yp_ref):
    for g, w in enumerate(POOL_WINDOWS):
        sl = slice(g * POOL_GROUP_DIM, (g + 1) * POOL_GROUP_DIM)
        p = win_sum(g, w) / cnt(w) - u[:, sl]
        y = _dot(p.astype(BF16), pw_ref[g]) * pscale_ref[:, sl]
        yp_ref[:, sl] = y.astype(yp_ref.dtype)


def _inproj_prompt_kernel(x_ref, g1_ref, w_ref, cos_ref, slo_ref, shi_ref, qg_ref, kg_ref, mqg_ref,
                          pw_ref, pscale_ref,
                          u_ref, q_ref, k_ref, v_ref, qm_ref, yp_ref,
                          uext_ref, *, tm, tiles_per_seq):
    i = pl.program_id(0)
    t_in_seq = i % tiles_per_seq
    hb = _rms(x_ref[...], g1_ref[...]).astype(BF16)

    u = _dot(hb, w_ref[:, 0:D_POOL])
    u_ref[...] = u

    @pl.when(t_in_seq == 0)
    def _():
        uext_ref[0:POOL_HALO, :] = jnp.zeros((POOL_HALO, D_POOL), F32)

    uext_ref[POOL_HALO:POOL_HALO + tm, :] = u
    pos = t_in_seq * tm + lax.broadcasted_iota(jnp.int32, (tm, 1), 0)

    def win_sum(g, w):
        sl = slice(g * POOL_GROUP_DIM, (g + 1) * POOL_GROUP_DIM)
        s = uext_ref[POOL_HALO:POOL_HALO + tm, sl]
        for d in range(1, w):
            s = s + uext_ref[POOL_HALO - d:POOL_HALO - d + tm, sl]
        return s

    _pool_branch(u, win_sum, lambda w: jnp.minimum(w, pos + 1).astype(F32), pw_ref, pscale_ref, yp_ref)
    uext_ref[0:POOL_HALO, :] = uext_ref[tm:tm + POOL_HALO, :]

    cos, slo, shi = cos_ref[...], slo_ref[...], shi_ref[...]
    c0 = D_POOL
    qf = _dot(hb, w_ref[:, c0:c0 + D_ATTN])
    for h in range(N_HEADS):
        t = _rms(qf[:, h * HEAD_DIM:(h + 1) * HEAD_DIM], qg_ref[...])
        q_ref[0, h] = _rope(t, cos, slo, shi)
    c0 += D_ATTN
    kf = _dot(hb, w_ref[:, c0:c0 + D_ATTN])
    for h in range(N_HEADS):
        t = _rms(kf[:, h * HEAD_DIM:(h + 1) * HEAD_DIM], kg_ref[...])
        k_ref[0, h] = _rope(t, cos, slo, shi)
    c0 += D_ATTN
    vf = _dot(hb, w_ref[:, c0:c0 + D_ATTN])
    for h in range(N_HEADS):
        v_ref[0, h] = vf[:, h * HEAD_DIM:(h + 1) * HEAD_DIM]
    c0 += D_ATTN
    mf = _dot(hb, w_ref[:, c0:c0 + D_MEM])
    for h in range(MEM_HEADS):
        sl = slice(h * HEAD_DIM, (h + 1) * HEAD_DIM)
        qm_ref[:, sl] = _rms(mf[:, sl], mqg_ref[...]).astype(qm_ref.dtype)


def _inproj_prompt(x2d, g1, w_in_b, tables, qg, kg, mqg, pool_w_b, pool_scale, *, batch, seq, tm):
    n = x2d.shape[0]
    tiles_per_seq = seq // tm
    row = lambda i: (i, 0)
    const = lambda i: (0, 0)
    tab = pl.BlockSpec((tm, HEAD_DIM), lambda i: (i % tiles_per_seq, 0))
    head_major = pl.BlockSpec((1, N_HEADS, tm, HEAD_DIM), lambda i: (i // tiles_per_seq, 0, i % tiles_per_seq, 0))
    hm_shape = jax.ShapeDtypeStruct((batch, N_HEADS, seq, HEAD_DIM), F32)
    return pl.pallas_call(
        functools.partial(_inproj_prompt_kernel, tm=tm, tiles_per_seq=tiles_per_seq),
        grid=(n // tm,),
        in_specs=[
            pl.BlockSpec((tm, D_MODEL), row),
            pl.BlockSpec((1, D_MODEL), const),
            pl.BlockSpec((D_MODEL, D_QKV), const, pipeline_mode=pl.Buffered(1)),
            tab, tab, tab,
            pl.BlockSpec((1, HEAD_DIM), const), pl.BlockSpec((1, HEAD_DIM), const), pl.BlockSpec((1, HEAD_DIM), const),
            pl.BlockSpec((len(POOL_WINDOWS), POOL_GROUP_DIM, POOL_GROUP_DIM), lambda i: (0, 0, 0)),
            pl.BlockSpec((1, D_POOL), const),
        ],
        out_specs=[
            pl.BlockSpec((tm, D_POOL), row), head_major, head_major, head_major,
            pl.BlockSpec((tm, D_MEM), row), pl.BlockSpec((tm, D_POOL), row),
        ],
        out_shape=[
            jax.ShapeDtypeStruct((n, D_POOL), F32), hm_shape, hm_shape, hm_shape,
            jax.ShapeDtypeStruct((n, D_MEM), BF16), jax.ShapeDtypeStruct((n, D_POOL), BF16),
        ],
        scratch_shapes=[pltpu.VMEM((tm + POOL_HALO, D_POOL), F32)],
        compiler_params=_params("arbitrary"),
        name="inproj_prompt",
    )(x2d, g1, w_in_b, *tables, qg, kg, mqg, pool_w_b, pool_scale)


def _inproj_sample_kernel(x_ref, g1_ref, w_ref, cos_ref, slo_ref, shi_ref, qg_ref, kg_ref, mqg_ref,
                          pw_ref, pscale_ref, state_ref,
                          u_ref, q_ref, k_ref, v_ref, qm_ref, yp_ref, *, past_len):
    hb = _rms(x_ref[...], g1_ref[...]).astype(BF16)
    u = _dot(hb, w_ref[:, 0:D_POOL])
    u_ref[...] = u
    db = state_ref.shape[0]
    srow = lax.broadcasted_iota(jnp.int32, (db, POOL_STATE, POOL_GROUP_DIM), 1)

    def win_sum(g, w):
        sl = slice(g * POOL_GROUP_DIM, (g + 1) * POOL_GROUP_DIM)
        hist = jnp.where(srow >= POOL_STATE - (w - 1), state_ref[:, :, sl], 0.0)
        return u[:, sl] + jnp.sum(hist, axis=1)

    _pool_branch(u, win_sum, lambda w: float(min(w, past_len + 1)), pw_ref, pscale_ref, yp_ref)

    cos, slo, shi = cos_ref[...], slo_ref[...], shi_ref[...]
    c0 = D_POOL
    qf = _dot(hb, w_ref[:, c0:c0 + D_ATTN])
    c0 += D_ATTN
    kf = _dot(hb, w_ref[:, c0:c0 + D_ATTN])
    for h in range(N_HEADS):
        sl = slice(h * HEAD_DIM, (h + 1) * HEAD_DIM)
        q_ref[:, sl] = _rope(_rms(qf[:, sl], qg_ref[...]), cos, slo, shi)
        k_ref[:, sl] = _rope(_rms(kf[:, sl], kg_ref[...]), cos, slo, shi)
    c0 += D_ATTN
    v_ref[...] = _dot(hb, w_ref[:, c0:c0 + D_ATTN])
    c0 += D_ATTN
    mf = _dot(hb, w_ref[:, c0:c0 + D_MEM])
    for h in range(MEM_HEADS):
        sl = slice(h * HEAD_DIM, (h + 1) * HEAD_DIM)
        qm_ref[:, sl] = _rms(mf[:, sl], mqg_ref[...]).astype(qm_ref.dtype)


def _inproj_sample(x2d, g1, w_in_b, tables, qg, kg, mqg, pool_w_b, pool_scale, state_pool, *, past_len):
    db = x2d.shape[0]
    full = lambda shape: pl.BlockSpec(shape, lambda i: (0,) * len(shape))
    return pl.pallas_call(
        functools.partial(_inproj_sample_kernel, past_len=past_len),
        grid=(1,),
        in_specs=[
            full((db, D_MODEL)), full((1, D_MODEL)),
            pl.BlockSpec((D_MODEL, D_QKV), lambda i: (0, 0), pipeline_mode=pl.Buffered(1)),
            full((db, HEAD_DIM)), full((db, HEAD_DIM)), full((db, HEAD_DIM)),
            full((1, HEAD_DIM)), full((1, HEAD_DIM)), full((1, HEAD_DIM)),
            full((len(POOL_WINDOWS), POOL_GROUP_DIM, POOL_GROUP_DIM)), full((1, D_POOL)),
            full((db, POOL_STATE, D_POOL)),
        ],
        out_specs=[full((db, D_POOL)), full((db, D_ATTN)), full((db, D_ATTN)), full((db, D_ATTN)),
                   full((db, D_MEM)), full((db, D_POOL))],
        out_shape=[jax.ShapeDtypeStruct((db, D_POOL), F32), jax.ShapeDtypeStruct((db, D_ATTN), F32),
                   jax.ShapeDtypeStruct((db, D_ATTN), F32), jax.ShapeDtypeStruct((db, D_ATTN), F32),
                   jax.ShapeDtypeStruct((db, D_MEM), BF16), jax.ShapeDtypeStruct((db, D_POOL), BF16)],
        compiler_params=_params("arbitrary"),
        name="inproj_sample",
    )(x2d, g1, w_in_b, *tables, qg, kg, mqg, pool_w_b, pool_scale, state_pool)


def _moba_prompt_kernel(q_ref, k_ref, v_ref, o_ref,
                        qa_ref, ka_ref, vb_ref, km_ref, gs_ref, bt_ref, tri_ref, s_ref, p_ref, *, seq):
    nb = seq // MOBA_BLOCK
    nbp = -(-nb // SUBLANES) * SUBLANES
    shift = MOBA_BLOCK.bit_length() - 1
    q = q_ref[0, 0]
    k = k_ref[0, 0]
    row_blk = lax.shift_right_logical(lax.broadcasted_iota(jnp.int32, (seq, LANES), 0), shift)
    lane = lax.broadcasted_iota(jnp.int32, (seq, LANES), 1)
    ka_ref[:, 0:HEAD_DIM] = k.astype(BF16)
    ka_ref[:, HEAD_DIM:] = jnp.where(lane == row_blk, 1.0, 0.0).astype(BF16)
    vb_ref[...] = v_ref[0, 0].astype(BF16)
    qa_ref[:, 0:HEAD_DIM] = (q * (SCALE * LOG2E)).astype(BF16)

    km_ref[...] = jnp.zeros(km_ref.shape, F32)
    km_ref[0:nb, :] = jnp.sum(k.reshape(nb, MOBA_BLOCK, HEAD_DIM), axis=1) * (1.0 / MOBA_BLOCK)
    gst = _dot_t(km_ref[...].astype(BF16), q.astype(BF16))
    jidx = lax.broadcasted_iota(jnp.int32, (nbp, seq), 0)
    qblk = lax.shift_right_logical(lax.broadcasted_iota(jnp.int32, (nbp, seq), 1), shift)
    past = jidx < qblk
    gs = jnp.where(past, gst[0:nbp, :], NEG)
    gs_ref[...] = gs
    rank = jnp.zeros((nbp, seq), F32)
    for jp in range(nb):
        other = gs_ref[jp:jp + 1, :]
        rank = rank + jnp.where(jidx > jp, jnp.where(other >= gs, 1.0, 0.0), jnp.where(other > gs, 1.0, 0.0))
    bt_ref[...] = jnp.zeros(bt_ref.shape, F32)
    bt_ref[0:nbp, :] = jnp.where(past, jnp.where(rank < MOBA_TOPK, 0.0, NEG), jnp.where(jidx == qblk, 0.0, NEG))
    for c in range(seq // MOBA_BLOCK):
        rows = slice(c * MOBA_BLOCK, (c + 1) * MOBA_BLOCK)
        qa_ref[rows, HEAD_DIM:] = bt_ref[:, rows].T.astype(BF16)

    tri_ref[...] = jnp.where(lax.broadcasted_iota(jnp.int32, tri_ref.shape, 1)
                             <= lax.broadcasted_iota(jnp.int32, tri_ref.shape, 0), 0.0, NEG)

    for bq in range(nb):
        par = bq % 2
        n_keys = bq + 1
        rows = slice(bq * MOBA_BLOCK, (bq + 1) * MOBA_BLOCK)
        qa = qa_ref[rows, :]
        m_part = None
        for j in range(n_keys):
            s = _dot_t(qa, ka_ref[j * MOBA_BLOCK:(j + 1) * MOBA_BLOCK, :])
            if j == bq:
                s = s + tri_ref[...]
            s_ref[par, j] = s
            mj = jnp.maximum(s[:, :LANES], s[:, LANES:])
            m_part = mj if m_part is None else jnp.maximum(m_part, mj)
        m = jnp.broadcast_to(jnp.max(m_part, axis=-1, keepdims=True), (MOBA_BLOCK, LANES))
        l_part = jnp.zeros((MOBA_BLOCK, LANES), F32)
        for j in range(n_keys):
            for half in range(MOBA_BLOCK // LANES):
                cols = slice(half * LANES, (half + 1) * LANES)
                p = jnp.exp2(s_ref[par, j, :, cols] - m)
                l_part = l_part + p
                p_ref[par, :, j * MOBA_BLOCK + half * LANES:j * MOBA_BLOCK + (half + 1) * LANES] = p.astype(BF16)
        l = jnp.sum(l_part, axis=-1, keepdims=True)
        o = _dot(p_ref[par, :, 0:n_keys * MOBA_BLOCK], vb_ref[0:n_keys * MOBA_BLOCK, :]) / l
        o_ref[rows, :] = o.astype(o_ref.dtype)


def _moba_prompt(q, k, v):
    b, h, s, _ = q.shape
    nb = s // MOBA_BLOCK
    assert nb <= LANES
    nbp = -(-nb // SUBLANES) * SUBLANES
    blk = pl.BlockSpec((1, 1, s, HEAD_DIM), lambda bi, hi: (bi, hi, 0, 0))
    return pl.pallas_call(
        functools.partial(_moba_prompt_kernel, seq=s),
        grid=(b, h),
        in_specs=[blk, blk, blk],
        out_specs=pl.BlockSpec((s, HEAD_DIM), lambda bi, hi: (bi, hi)),
        out_shape=jax.ShapeDtypeStruct((b * s, h * HEAD_DIM), BF16),
        scratch_shapes=[pltpu.VMEM((s, 2 * HEAD_DIM), BF16), pltpu.VMEM((s, 2 * HEAD_DIM), BF16),
                        pltpu.VMEM((s, HEAD_DIM), BF16),
                        pltpu.VMEM((LANES, HEAD_DIM), F32), pltpu.VMEM((nbp, s), F32), pltpu.VMEM((LANES, s), F32),
                        pltpu.VMEM((MOBA_BLOCK, MOBA_BLOCK), F32),
                        pltpu.VMEM((2, nb, MOBA_BLOCK, MOBA_BLOCK), F32), pltpu.VMEM((2, MOBA_BLOCK, s), BF16)],
        compiler_params=_params("arbitrary", "arbitrary"),
        name="moba_prompt",
    )(q, k, v)


def _mem_attend_kernel(q_ref, k_ref, v_ref, o_ref):
    for h in range(MEM_HEADS):
        sl = slice(h * HEAD_DIM, (h + 1) * HEAD_DIM)
        qs = (q_ref[:, sl].astype(F32) * SCALE).astype(BF16)
        s = _dot_t(qs, k_ref[0][:, sl].astype(BF16))
        m = jnp.max(s, axis=-1, keepdims=True)
        p = jnp.exp(s - m)
        l = jnp.sum(p, axis=-1, keepdims=True)
        o = _dot(p.astype(BF16), v_ref[0][:, sl].astype(BF16)) / l
        o_ref[:, sl] = o.astype(o_ref.dtype)


def _mem_attend(qm, mem_k, mem_v, *, rows_per_seq, tq):
    n = qm.shape[0]
    m_tok = mem_k.shape[1]
    tiles = rows_per_seq // tq
    kv = pl.BlockSpec((1, m_tok, D_MEM), lambda i: (i // tiles, 0, 0))
    return pl.pallas_call(
        _mem_attend_kernel,
        grid=(n // tq,),
        in_specs=[pl.BlockSpec((tq, D_MEM), lambda i: (i, 0)), kv, kv],
        out_specs=pl.BlockSpec((tq, D_MEM), lambda i: (i, 0)),
        out_shape=jax.ShapeDtypeStruct((n, D_MEM), BF16),
        compiler_params=_params("arbitrary"),
        name="mem_attend",
    )(qm, mem_k, mem_v)


def _mem_kv_kernel(x_ref, w_ref, g_ref, k_ref, v_ref):
    kv = _dot(x_ref[...].astype(BF16), w_ref[...])
    for h in range(MEM_HEADS):
        sl = slice(h * HEAD_DIM, (h + 1) * HEAD_DIM)
        k_ref[:, sl] = _rms(kv[:, sl], g_ref[...])
    v_ref[...] = kv[:, D_MEM:]


def _mem_kv(mem2d, w_b, g, *, tm):
    n = mem2d.shape[0]
    out = jax.ShapeDtypeStruct((n, D_MEM), F32)
    return pl.pallas_call(
        _mem_kv_kernel,
        grid=(n // tm,),
        in_specs=[pl.BlockSpec((tm, D_MODEL), lambda i: (i, 0)),
                  pl.BlockSpec((D_MODEL, 2 * D_MEM), lambda i: (0, 0)),
                  pl.BlockSpec((1, HEAD_DIM), lambda i: (0, 0))],
        out_specs=[pl.BlockSpec((tm, D_MEM), lambda i: (i, 0))] * 2,
        out_shape=[out, out],
        compiler_params=_params("arbitrary"),
        name="mem_kv",
    )(mem2d, w_b, g)


def _merge_kernel(x_ref, g1_ref, yp_ref, ya_ref, ym_ref, wg0_ref, wg1_ref, wg2_ref,
                  wbp_ref, wba_ref, wbm_ref, wo_ref, o_ref, hb_ref, acc_ref):
    c = pl.program_id(1)

    @pl.when(c == 0)
    def _():
        hb_ref[...] = _rms(x_ref[...], g1_ref[...]).astype(BF16)
        acc_ref[...] = jnp.zeros(acc_ref.shape, F32)

    hb = hb_ref[...]
    merged = (jax.nn.sigmoid(_dot(hb, wg0_ref[...])) * _dot(yp_ref[...].astype(BF16), wbp_ref[...])
              + jax.nn.sigmoid(_dot(hb, wg1_ref[...])) * _dot(ya_ref[...].astype(BF16), wba_ref[...])
              + jax.nn.sigmoid(_dot(hb, wg2_ref[...])) * _dot(ym_ref[...].astype(BF16), wbm_ref[...]))
    acc_ref[...] += _dot(merged.astype(BF16), wo_ref[...])

    @pl.when(c == pl.num_programs(1) - 1)
    def _():
        o_ref[...] = x_ref[...] + acc_ref[...]


def _merge(x2d, g1, yp, ya, ym, w_in_b, wbp_b, wba_b, wbm_b, wo_b, *, tm):
    n = x2d.shape[0]
    tc = MERGE_CHUNK
    n_chunks = D_MODEL // tc
    gate0 = D_QKV // tc
    row = lambda i, c: (i, 0)

    def gate_spec(branch):
        return pl.BlockSpec((D_MODEL, tc), lambda i, c: (0, gate0 + branch * n_chunks + c))

    col = lambda rows: pl.BlockSpec((rows, tc), lambda i, c: (0, c))
    return pl.pallas_call(
        _merge_kernel,
        grid=(n // tm, n_chunks),
        in_specs=[pl.BlockSpec((tm, D_MODEL), row), pl.BlockSpec((1, D_MODEL), lambda i, c: (0, 0)),
                  pl.BlockSpec((tm, D_POOL), row), pl.BlockSpec((tm, D_ATTN), row), pl.BlockSpec((tm, D_MEM), row),
                  gate_spec(0), gate_spec(1), gate_spec(2),
                  col(D_POOL), col(D_ATTN), col(D_MEM),
                  pl.BlockSpec((tc, D_MODEL), lambda i, c: (c, 0))],
        out_specs=pl.BlockSpec((tm, D_MODEL), row),
        out_shape=jax.ShapeDtypeStruct((n, D_MODEL), F32),
        scratch_shapes=[pltpu.VMEM((tm, D_MODEL), BF16), pltpu.VMEM((tm, D_MODEL), F32)],
        compiler_params=_params("arbitrary", "arbitrary"),
        name="merge",
    )(x2d, g1, yp, ya, ym, w_in_b, w_in_b, w_in_b, wbp_b, wba_b, wbm_b, wo_b)


def _ffn_act(ua, ug, a_m1, a_m2, g_m1, g_m2, cwa_ref, cwg_ref, cba_ref, cbg_ref):
    ca = cba_ref[...] + cwa_ref[0:1, :] * a_m2 + cwa_ref[1:2, :] * a_m1 + cwa_ref[2:3, :] * ua
    cg = cbg_ref[...] + cwg_ref[0:1, :] * g_m2 + cwg_ref[1:2, :] * g_m1 + cwg_ref[2:3, :] * ug
    return (jax.nn.silu(ca) * cg).astype(BF16)


def _ffn_seq_kernel(xu_ref, xr_ref, g2_ref, wua_ref, wug_ref, cwa_ref, cwg_ref, cba_ref, cbg_ref, wd_ref,
                    o_ref, sta_ref, stg_ref,
                    hb_ref, acc_ref, act_ref, exta_ref, extg_ref, carry_ref,
                    *, tm, tiles_per_seq, n_chunks, n_steps):
    t = pl.program_id(0)
    tu = jnp.minimum(t, n_steps - 1)
    i, c = tu // n_chunks, tu % n_chunks
    n_prev = CONV_WIDTH - 1
    lo = CONV_HALO - n_prev

    @pl.when(t == 0)
    def _():
        acc_ref[...] = jnp.zeros(acc_ref.shape, F32)
        act_ref[1] = jnp.zeros(act_ref.shape[1:], BF16)

    @pl.when(c == 0)
    def _():
        hb_ref[...] = _rms(xu_ref[...], g2_ref[...]).astype(BF16)

    @pl.when(i % tiles_per_seq == 0)
    def _():
        carry_ref[c] = jnp.zeros(carry_ref.shape[1:], F32)

    acc_ref[...] += _dot(act_ref[(t + 1) % 2], wd_ref[...])

    hb = hb_ref[...]
    ua = _dot(hb, wua_ref[...])
    ug = _dot(hb, wug_ref[...])
    exta_ref[lo:CONV_HALO, :] = carry_ref[c, 0:n_prev, :]
    extg_ref[lo:CONV_HALO, :] = carry_ref[c, n_prev:2 * n_prev, :]
    exta_ref[CONV_HALO:CONV_HALO + tm, :] = ua
    extg_ref[CONV_HALO:CONV_HALO + tm, :] = ug
    a_last = ua[tm - n_prev:tm, :]
    g_last = ug[tm - n_prev:tm, :]
    carry_ref[c, 0:n_prev, :] = a_last
    carry_ref[c, n_prev:2 * n_prev, :] = g_last
    act_ref[t % 2] = _ffn_act(
        ua, ug,
        exta_ref[CONV_HALO - 1:CONV_HALO - 1 + tm, :], exta_ref[CONV_HALO - 2:CONV_HALO - 2 + tm, :],
        extg_ref[CONV_HALO - 1:CONV_HALO - 1 + tm, :], extg_ref[CONV_HALO - 2:CONV_HALO - 2 + tm, :],
        cwa_ref, cwg_ref, cba_ref, cbg_ref)

    @pl.when(i % tiles_per_seq == tiles_per_seq - 1)
    def _():
        sta_ref[i // tiles_per_seq, c] = a_last
        stg_ref[i // tiles_per_seq, c] = g_last

    @pl.when((t > 0) & (t % n_chunks == 0))
    def _():
        o_ref[...] = xr_ref[...] + acc_ref[...]
        acc_ref[...] = jnp.zeros(acc_ref.shape, F32)


def _ffn_step_kernel(x_ref, g2_ref, wua_ref, wug_ref, cwa_ref, cwg_ref, cba_ref, cbg_ref, wd_ref,
                     am2_ref, gm2_ref, am1_ref, gm1_ref,
                     o_ref, ua_ref, ug_ref, hb_ref, acc_ref):
    c = pl.program_id(1)

    @pl.when(c == 0)
    def _():
        hb_ref[...] = _rms(x_ref[...], g2_ref[...]).astype(BF16)
        acc_ref[...] = jnp.zeros(acc_ref.shape, F32)

    hb = hb_ref[...]
    ua = _dot(hb, wua_ref[...])
    ug = _dot(hb, wug_ref[...])
    ua_ref[...] = ua
    ug_ref[...] = ug
    act = _ffn_act(ua, ug, am1_ref[...], am2_ref[...], gm1_ref[...], gm2_ref[...], cwa_ref, cwg_ref, cba_ref, cbg_ref)
    acc_ref[...] += _dot(act, wd_ref[...])

    @pl.when(c == pl.num_programs(1) - 1)
    def _():
        o_ref[...] = x_ref[...] + acc_ref[...]


def _ffn(x2d, g2, w_up_b, conv_w, conv_b2, w_down_b, *, tm, batch=None, seq=None, prev=None):
    n = x2d.shape[0]
    tc = FFN_CHUNK
    n_chunks = D_FF // tc
    row = lambda i, c: (i, 0)
    a_col = lambda rows: pl.BlockSpec((rows, tc), lambda i, c: (0, c))
    g_col = lambda rows: pl.BlockSpec((rows, tc), lambda i, c: (0, n_chunks + c))
    common_in = [pl.BlockSpec((tm, D_MODEL), row), pl.BlockSpec((1, D_MODEL), lambda i, c: (0, 0)),
                 a_col(D_MODEL), g_col(D_MODEL), a_col(CONV_WIDTH), g_col(CONV_WIDTH), a_col(1), g_col(1),
                 pl.BlockSpec((tc, D_MODEL), lambda i, c: (c, 0))]
    common_args = (x2d, g2, w_up_b, w_up_b, conv_w, conv_w, conv_b2, conv_b2, w_down_b)
    common_scratch = [pltpu.VMEM((tm, D_MODEL), BF16), pltpu.VMEM((tm, D_MODEL), F32)]
    y_spec = pl.BlockSpec((tm, D_MODEL), row)
    y_shape = jax.ShapeDtypeStruct((n, D_MODEL), F32)
    if prev is None:
        tiles_per_seq = seq // tm
        n_prev = CONV_WIDTH - 1
        n_steps = (n // tm) * n_chunks
        up_tile = lambda t: jnp.minimum(t, n_steps - 1) // n_chunks
        up_chunk = lambda t: jnp.minimum(t, n_steps - 1) % n_chunks
        down_tile = lambda t: jnp.maximum(t - 1, 0) // n_chunks
        down_chunk = lambda t: jnp.maximum(t - 1, 0) % n_chunks
        a_up = lambda rows: pl.BlockSpec((rows, tc), lambda t: (0, up_chunk(t)))
        g_up = lambda rows: pl.BlockSpec((rows, tc), lambda t: (0, n_chunks + up_chunk(t)))
        st_spec = pl.BlockSpec((batch, n_chunks, n_prev, tc), lambda t: (0, 0, 0, 0))
        st_shape = jax.ShapeDtypeStruct((batch, n_chunks, n_prev, tc), F32)
        y, sta, stg = pl.pallas_call(
            functools.partial(_ffn_seq_kernel, tm=tm, tiles_per_seq=tiles_per_seq, n_chunks=n_chunks, n_steps=n_steps),
            grid=(n_steps + 1,),
            in_specs=[pl.BlockSpec((tm, D_MODEL), lambda t: (up_tile(t), 0)),
                      pl.BlockSpec((tm, D_MODEL), lambda t: (down_tile(t), 0)),
                      pl.BlockSpec((1, D_MODEL), lambda t: (0, 0)),
                      a_up(D_MODEL), g_up(D_MODEL), a_up(CONV_WIDTH), g_up(CONV_WIDTH), a_up(1), g_up(1),
                      pl.BlockSpec((tc, D_MODEL), lambda t: (down_chunk(t), 0))],
            out_specs=[pl.BlockSpec((tm, D_MODEL), lambda t: (down_tile(t), 0)), st_spec, st_spec],
            out_shape=[y_shape, st_shape, st_shape],
            scratch_shapes=common_scratch + [pltpu.VMEM((2, tm, tc), BF16),
                                             pltpu.VMEM((tm + CONV_HALO, tc), F32), pltpu.VMEM((tm + CONV_HALO, tc), F32),
                                             pltpu.VMEM((n_chunks, SUBLANES, tc), F32)],
            compiler_params=_params("arbitrary"),
            name="ffn_seq",
        )(x2d, x2d, g2, w_up_b, w_up_b, conv_w, conv_w, conv_b2, conv_b2, w_down_b)
        unchunk = lambda st: st.transpose(0, 2, 1, 3).reshape(batch, n_prev, D_FF)
        return y, unchunk(sta), unchunk(stg)
    prev2, prev1 = prev
    rows_a = pl.BlockSpec((tm, tc), lambda i, c: (i, c))
    rows_g = pl.BlockSpec((tm, tc), lambda i, c: (i, n_chunks + c))
    up_shape = jax.ShapeDtypeStruct((n, D_FF), F32)
    return pl.pallas_call(
        _ffn_step_kernel,
        grid=(n // tm, n_chunks),
        in_specs=common_in + [rows_a, rows_g, rows_a, rows_g],
        out_specs=[y_spec, rows_a, rows_a],
        out_shape=[y_shape, up_shape, up_shape],
        scratch_shapes=common_scratch,
        compiler_params=_params("arbitrary", "arbitrary"),
        name="ffn_step",
    )(*common_args, prev2, prev2, prev1, prev1)


def _page_copy(pt_ref, cache_ref, buf_ref, sem_ref, t, n_pages, ring):
    page = pt_ref[lax.div(t, n_pages), lax.rem(t, n_pages)]
    slot = lax.rem(t, ring)
    return pltpu.make_async_copy(cache_ref.at[0, page], buf_ref.at[slot], sem_ref.at[slot])


def _page_sums_kernel(pt_ref, cache_ref, o_ref, buf_ref, sem_ref, *, n_seq, n_pages, ring):
    total = n_seq * n_pages
    for t in range(ring):
        _page_copy(pt_ref, cache_ref, buf_ref, sem_ref, jnp.int32(t), n_pages, ring).start()

    def body(t, carry):
        _page_copy(pt_ref, cache_ref, buf_ref, sem_ref, t, n_pages, ring).wait()
        sums = jnp.sum(buf_ref[lax.rem(t, ring)], axis=1)
        b, p = lax.div(t, n_pages), lax.rem(t, n_pages)
        for h in range(N_HEADS):
            o_ref[b, h, pl.ds(p, 1), :] = sums[h:h + 1, :]

        @pl.when(t + ring < total)
        def _():
            _page_copy(pt_ref, cache_ref, buf_ref, sem_ref, t + ring, n_pages, ring).start()

        return carry

    lax.fori_loop(0, total, body, 0)


def _page_sums(page_table, cache_k):
    db, n_pages = page_table.shape
    ring = min(PAGE_RING, db * n_pages)
    out_shape = (db, N_HEADS, n_pages, HEAD_DIM)
    return pl.pallas_call(
        functools.partial(_page_sums_kernel, n_seq=db, n_pages=n_pages, ring=ring),
        grid_spec=pltpu.PrefetchScalarGridSpec(
            num_scalar_prefetch=1,
            grid=(1,),
            in_specs=[pl.BlockSpec(memory_space=pl.ANY)],
            out_specs=pl.BlockSpec(out_shape, lambda i, pt: (0, 0, 0, 0)),
            scratch_shapes=[pltpu.VMEM((ring, N_HEADS, PAGE_SIZE, HEAD_DIM), F32),
                            pltpu.SemaphoreType.DMA((ring,))],
        ),
        out_shape=jax.ShapeDtypeStruct(out_shape, F32),
        compiler_params=_params("arbitrary"),
        name="page_sums",
    )(page_table, cache_k)


def _block_gate_kernel(q_ref, ps_ref, o_ref, km_ref, *, n_blocks):
    km_ref[...] = jnp.zeros(km_ref.shape, F32)
    lane = lax.broadcasted_iota(jnp.int32, (SUBLANES, LANES), 1)
    for h in range(N_HEADS):
        even = ps_ref[0, h, pl.ds(0, n_blocks, stride=PAGES_PER_BLOCK), :]
        odd = ps_ref[0, h, pl.ds(1, n_blocks, stride=PAGES_PER_BLOCK), :]
        km_ref[0:n_blocks, :] = (even + odd) * (1.0 / MOBA_BLOCK)
        qh = jnp.broadcast_to(q_ref[0, :, h * HEAD_DIM:(h + 1) * HEAD_DIM], (SUBLANES, HEAD_DIM))
        gs = lax.dot_general(qh, km_ref[...], (((1,), (1,)), ((), ())), preferred_element_type=F32,
                             precision=lax.Precision.HIGHEST)
        gs = jnp.where(lane < n_blocks, gs, NEG)
        i0, i1, i2 = _top3(gs, n_blocks)
        picked = jnp.where(lane == 0, i0, jnp.where(lane == 1, i1, jnp.where(lane == 2, i2, 0)))
        o_ref[0, h:h + 1, :] = picked[0:1, :]


def _block_gate(q3, page_sums):
    db, _, n_pages, _ = page_sums.shape
    n_blocks = n_pages // PAGES_PER_BLOCK
    assert PAGES_PER_BLOCK == 2 and MOBA_TOPK <= n_blocks <= LANES
    return pl.pallas_call(
        functools.partial(_block_gate_kernel, n_blocks=n_blocks),
        grid=(db,),
        in_specs=[pl.BlockSpec((1, 1, D_ATTN), lambda b: (b, 0, 0)),
                  pl.BlockSpec((1, N_HEADS, n_pages, HEAD_DIM), lambda b: (b, 0, 0, 0))],
        out_specs=pl.BlockSpec((1, N_HEADS, LANES), lambda b: (b, 0, 0)),
        out_shape=jax.ShapeDtypeStruct((db, N_HEADS, LANES), jnp.int32),
        scratch_shapes=[pltpu.VMEM((LANES, HEAD_DIM), F32)],
        compiler_params=_params("arbitrary"),
        name="block_gate",
    )(q3, page_sums)


N_SEL_PAGES = MOBA_TOPK * PAGES_PER_BLOCK


def _sel_copy(pt_ref, sel_ref, cache_ref, buf_ref, sem_ref, b, h, s):
    blk = sel_ref[b, h * MOBA_TOPK + s // PAGES_PER_BLOCK]
    page = pt_ref[b, blk * PAGES_PER_BLOCK + s % PAGES_PER_BLOCK]
    return pltpu.make_async_copy(cache_ref.at[0, page, h], buf_ref.at[h, s], sem_ref.at[h])


def _moba_sample_kernel(pt_ref, sel_ref, q_ref, kn_ref, vn_ref, ck_ref, cv_ref, o_ref,
                        kbuf_ref, vbuf_ref, ksem_ref, vsem_ref):
    b = pl.program_id(0)
    for h in range(N_HEADS):
        for s in range(N_SEL_PAGES):
            _sel_copy(pt_ref, sel_ref, ck_ref, kbuf_ref, ksem_ref, b, h, s).start()
            _sel_copy(pt_ref, sel_ref, cv_ref, vbuf_ref, vsem_ref, b, h, s).start()
    for h in range(N_HEADS):
        for s in range(N_SEL_PAGES):
            _sel_copy(pt_ref, sel_ref, ck_ref, kbuf_ref, ksem_ref, b, h, s).wait()
            _sel_copy(pt_ref, sel_ref, cv_ref, vbuf_ref, vsem_ref, b, h, s).wait()
        sl = slice(h * HEAD_DIM, (h + 1) * HEAD_DIM)
        qs = q_ref[0, :, sl] * SCALE
        q8 = jnp.broadcast_to(qs, (SUBLANES, HEAD_DIM)).astype(BF16)
        kk = kbuf_ref[h].reshape(N_SEL_PAGES * PAGE_SIZE, HEAD_DIM).astype(BF16)
        vv = vbuf_ref[h].reshape(N_SEL_PAGES * PAGE_SIZE, HEAD_DIM).astype(BF16)
        s_sel = _dot_t(q8, kk)
        s_own = jnp.sum(qs * kn_ref[0, :, sl], axis=-1, keepdims=True)
        m = jnp.maximum(jnp.max(s_sel, axis=-1, keepdims=True), s_own)
        p = jnp.exp(s_sel - m)
        p_own = jnp.exp(s_own - m)
        l = jnp.sum(p, axis=-1, keepdims=True) + p_own
        o = (_dot(p.astype(BF16), vv) + p_own * vn_ref[0, :, sl]) / l
        o_ref[0, :, sl] = o[0:1, :].astype(o_ref.dtype)


def _moba_sample(page_table, sel, q3, k3, v3, cache_k, cache_v):
    db = q3.shape[0]
    tok = pl.BlockSpec((1, 1, D_ATTN), lambda b, pt, sl: (b, 0, 0))
    return pl.pallas_call(
        _moba_sample_kernel,
        grid_spec=pltpu.PrefetchScalarGridSpec(
            num_scalar_prefetch=2,
            grid=(db,),
            in_specs=[tok, tok, tok, pl.BlockSpec(memory_space=pl.ANY), pl.BlockSpec(memory_space=pl.ANY)],
            out_specs=tok,
            scratch_shapes=[pltpu.VMEM((N_HEADS, N_SEL_PAGES, PAGE_SIZE, HEAD_DIM), F32),
                            pltpu.VMEM((N_HEADS, N_SEL_PAGES, PAGE_SIZE, HEAD_DIM), F32),
                            pltpu.SemaphoreType.DMA((N_HEADS,)), pltpu.SemaphoreType.DMA((N_HEADS,))],
        ),
        out_shape=jax.ShapeDtypeStruct((db, 1, D_ATTN), F32),
        compiler_params=_params("arbitrary"),
        name="moba_sample",
    )(page_table, sel, q3, k3, v3, cache_k, cache_v)


def _row_tile(n, pref):
    return pref if n % pref == 0 else n


def kernel(x_prompt, x_sample, mem_prompt, cache_k, cache_v, page_table, cache_mem_k, cache_mem_v, state_pool, state_conv, norm1_g, w_in, q_norm_g, k_norm_g, mem_q_norm_g, mem_k_norm_g, w_mem_kv, pool_w, pool_scale, w_branch_pool, w_branch_attn, w_branch_mem, w_out, norm2_g, w_up, conv_w, conv_b, w_down):
    b, s, _ = x_prompt.shape
    db, t, _ = x_sample.shape
    assert t == 1 and norm1_g.shape[0] == 1 and s % MOBA_BLOCK == 0
    n_pages = page_table.shape[1]
    assert n_pages % PAGES_PER_BLOCK == 0
    m_tok = mem_prompt.shape[1]

    w_in_b = w_in[0].astype(BF16)
    wbp_b, wba_b, wbm_b = (w[0].astype(BF16) for w in (w_branch_pool, w_branch_attn, w_branch_mem))
    wo_b, w_up_b, w_down_b = w_out[0].astype(BF16), w_up[0].astype(BF16), w_down[0].astype(BF16)
    w_mem_b, pool_w_b = w_mem_kv[0].astype(BF16), pool_w[0].astype(BF16)
    g1, g2 = norm1_g, norm2_g
    conv_b2 = conv_b
    norm_args = (q_norm_g, k_norm_g, mem_q_norm_g, pool_w_b, pool_scale)

    n = b * s
    x2d = x_prompt.reshape(n, D_MODEL)
    tm = _row_tile(s, 512)
    tables_p = _rope_tables(jnp.arange(s, dtype=jnp.int32))
    u_p, q_p, k_p, v_p, qm_p, yp_p = _inproj_prompt(x2d, g1, w_in_b, tables_p, *norm_args, batch=b, seq=s, tm=tm)
    mk2d, mv2d = _mem_kv(mem_prompt.reshape(b * m_tok, D_MODEL), w_mem_b, mem_k_norm_g, tm=m_tok)
    ya_p = _moba_prompt(q_p, k_p, v_p)
    ym_p = _mem_attend(qm_p, mk2d.reshape(b, m_tok, D_MEM), mv2d.reshape(b, m_tok, D_MEM),
                       rows_per_seq=s, tq=_row_tile(s, 1024))
    x1_p = _merge(x2d, g1, yp_p, ya_p, ym_p, w_in_b, wbp_b, wba_b, wbm_b, wo_b, tm=tm)
    y_p, sta_p, stg_p = _ffn(x1_p, g2, w_up_b, conv_w[0], conv_b2, w_down_b, tm=tm, batch=b, seq=s)

    xs2d = x_sample.reshape(db, D_MODEL)
    past_len = n_pages * PAGE_SIZE
    pos_s = jnp.full((db,), past_len, jnp.int32)
    u_s, q_s, k_s, v_s, qm_s, yp_s = _inproj_sample(xs2d, g1, w_in_b, _rope_tables(pos_s), *norm_args, state_pool[0],
                                                    past_len=past_len)
    q3, k3, v3 = (a.reshape(db, 1, D_ATTN) for a in (q_s, k_s, v_s))
    sums = _page_sums(page_table, cache_k)
    sel = _block_gate(q3, sums)[:, :, :MOBA_TOPK].reshape(db, N_HEADS * MOBA_TOPK)
    ya_s = _moba_sample(page_table, sel, q3, k3, v3, cache_k, cache_v).reshape(db, D_ATTN)
    qm_rep = jnp.broadcast_to(qm_s[:, None, :], (db, BF16_ROWS, D_MEM)).reshape(db * BF16_ROWS, D_MEM)
    ym_s = _mem_attend(qm_rep, cache_mem_k[0].reshape(db, m_tok, D_MEM), cache_mem_v[0].reshape(db, m_tok, D_MEM),
                       rows_per_seq=BF16_ROWS, tq=BF16_ROWS).reshape(db, BF16_ROWS, D_MEM)[:, 0]
    x1_s = _merge(xs2d, g1, yp_s, ya_s, ym_s, w_in_b, wbp_b, wba_b, wbm_b, wo_b, tm=db)
    y_s, ua_s, ug_s = _ffn(x1_s, g2, w_up_b, conv_w[0], conv_b2, w_down_b, tm=db,
                           prev=(state_conv[0, :, 0], state_conv[0, :, 1]))

    up_s = jnp.concatenate([ua_s, ug_s], axis=-1)
    return (
        y_p.reshape(b, s, D_MODEL),
        y_s.reshape(db, 1, D_MODEL),
        k_p[None], v_p[None],
        k_s.reshape(1, db, N_HEADS, 1, HEAD_DIM), v_s.reshape(1, db, N_HEADS, 1, HEAD_DIM),
        mk2d.reshape(1, b, m_tok, MEM_HEADS, HEAD_DIM), mv2d.reshape(1, b, m_tok, MEM_HEADS, HEAD_DIM),
        u_p.reshape(b, s, D_POOL)[None, :, s - POOL_STATE:],
        jnp.concatenate([state_pool[0][:, 1:], u_s[:, None]], axis=1)[None],
        jnp.concatenate([sta_p, stg_p], axis=-1)[None],
        jnp.concatenate([state_conv[0][:, 1:], up_s[:, None]], axis=1)[None],
    )
```

```python
import functools

import jax
import jax.numpy as jnp
import numpy as np
from jax import lax
from jax.experimental import pallas as pl
from jax.experimental.pallas import tpu as pltpu

F32 = jnp.float32
BF16 = jnp.bfloat16

D_MODEL = 2048
HEAD_DIM = 128
N_HEADS = 8
MEM_HEADS = 4
POOL_WINDOWS = (2, 4, 8, 16)
POOL_GROUP_DIM = 128
D_POOL = len(POOL_WINDOWS) * POOL_GROUP_DIM
POOL_STATE = max(POOL_WINDOWS) - 1
D_ATTN = N_HEADS * HEAD_DIM
D_MEM = MEM_HEADS * HEAD_DIM
D_QKV = D_POOL + 3 * D_ATTN + D_MEM
N_BRANCH = 3
MOBA_BLOCK = 256
MOBA_TOPK = 3
PAGE_SIZE = 128
PAGES_PER_BLOCK = MOBA_BLOCK // PAGE_SIZE
ROPE_THETA = 500000.0
ROPE_DIM = HEAD_DIM // 4
ROPE_HALF = ROPE_DIM // 2
D_FF = 5632
CONV_WIDTH = 3
EPS = 1e-6
NEG = -1e30
SCALE = HEAD_DIM ** -0.5
LOG2E = 1.4426950408889634

LANES = 128
SUBLANES = 8
BF16_ROWS = 2 * SUBLANES
POOL_HALO = 16
CONV_HALO = SUBLANES
VMEM_LIMIT = 56 * 1024 * 1024
FFN_CHUNK = 512
MERGE_CHUNK = 512


def _params(*sem):
    return pltpu.CompilerParams(dimension_semantics=sem, vmem_limit_bytes=VMEM_LIMIT)


def _rms(x, g):
    ms = jnp.mean(x * x, axis=-1, keepdims=True)
    return x * lax.rsqrt(ms + EPS) * g


def _rope(t, cos, sin_lo, sin_hi):
    return (t * cos + pltpu.roll(t, ROPE_HALF, 1) * sin_hi
            + pltpu.roll(t, HEAD_DIM - ROPE_HALF, 1) * sin_lo)


def _rope_tables(pos):
    half = ROPE_HALF
    inv = ROPE_THETA ** (-jnp.arange(half, dtype=F32) * (2.0 / ROPE_DIM))
    ang = pos.astype(F32)[:, None] * inv[None, :]
    cos, sin = jnp.cos(ang), jnp.sin(ang)
    n = pos.shape[0]
    pad = jnp.zeros((n, HEAD_DIM - ROPE_DIM), F32)
    zero = jnp.zeros((n, half), F32)
    cos_t = jnp.concatenate([cos, cos, pad + 1.0], axis=1)
    sin_lo = jnp.concatenate([-sin, zero, pad], axis=1)
    sin_hi = jnp.concatenate([zero, sin, pad], axis=1)
    return cos_t, sin_lo, sin_hi


def _dot(a, b):
    return jnp.dot(a, b, preferred_element_type=F32)


def _dot_t(a, b):
    return lax.dot_general(a, b, (((1,), (1,)), ((), ())), preferred_element_type=F32)


def _top3(gs, n_valid):
    lane = lax.broadcasted_iota(jnp.int32, gs.shape, 1)
    out = []
    for r in range(MOBA_TOPK):
        mx = jnp.max(gs, axis=-1, keepdims=True)
        idx = jnp.min(jnp.where(gs == mx, lane, LANES), axis=-1, keepdims=True)
        out.append(jnp.where(r < n_valid, idx, -1))
        gs = jnp.where(lane == idx, -jnp.inf, gs)
    return out


def _pool_branch(u, win_sum, cnt, pw_ref, pscale_ref, yp_ref):
    for g, w in enumerate(POOL_WINDOWS):
        sl = slice(g * POOL_GROUP_DIM, (g + 1) * POOL_GROUP_DIM)
        p = win_sum(g, w) / cnt(w) - u[:, sl]
        y = _dot(p.astype(BF16), pw_ref[g]) * pscale_ref[:, sl]
        yp_ref[:, sl] = y.astype(yp_ref.dtype)


def _inproj_prompt_kernel(x_ref, g1_ref, w_ref, cos_ref, slo_ref, shi_ref, qg_ref, kg_ref, mqg_ref,
                          pw_ref, pscale_ref,
                          u_ref, q_ref, k_ref, v_ref, qm_ref, yp_ref,
                          uext_ref, *, tm, tiles_per_seq):
    i = pl.program_id(0)
    t_in_seq = i % tiles_per_seq
    hb = _rms(x_ref[...], g1_ref[...]).astype(BF16)

    u = _dot(hb, w_ref[:, 0:D_POOL])
    u_ref[...] = u

    @pl.when(t_in_seq == 0)
    def _():
        uext_ref[0:POOL_HALO, :] = jnp.zeros((POOL_HALO, D_POOL), F32)

    uext_ref[POOL_HALO:POOL_HALO + tm, :] = u
    pos = t_in_seq * tm + lax.broadcasted_iota(jnp.int32, (tm, 1), 0)

    def win_sum(g, w):
        sl = slice(g * POOL_GROUP_DIM, (g + 1) * POOL_GROUP_DIM)
        s = uext_ref[POOL_HALO:POOL_HALO + tm, sl]
        for d in range(1, w):
            s = s + uext_ref[POOL_HALO - d:POOL_HALO - d + tm, sl]
        return s

    _pool_branch(u, win_sum, lambda w: jnp.minimum(w, pos + 1).astype(F32), pw_ref, pscale_ref, yp_ref)
    uext_ref[0:POOL_HALO, :] = uext_ref[tm:tm + POOL_HALO, :]

    cos, slo, shi = cos_ref[...], slo_ref[...], shi_ref[...]
    c0 = D_POOL
    qf = _dot(hb, w_ref[:, c0:c0 + D_ATTN])
    for h in range(N_HEADS):
        t = _rms(qf[:, h * HEAD_DIM:(h + 1) * HEAD_DIM], qg_ref[...])
        q_ref[0, h] = _rope(t, cos, slo, shi)
    c0 += D_ATTN
    kf = _dot(hb, w_ref[:, c0:c0 + D_ATTN])
    for h in range(N_HEADS):
        t = _rms(kf[:, h * HEAD_DIM:(h + 1) * HEAD_DIM], kg_ref[...])
        k_ref[0, h] = _rope(t, cos, slo, shi)
    c0 += D_ATTN
    vf = _dot(hb, w_ref[:, c0:c0 + D_ATTN])
    for h in range(N_HEADS):
        v_ref[0, h] = vf[:, h * HEAD_DIM:(h + 1) * HEAD_DIM]
    c0 += D_ATTN
    mf = _dot(hb, w_ref[:, c0:c0 + D_MEM])
    for h in range(MEM_HEADS):
        sl = slice(h * HEAD_DIM, (h + 1) * HEAD_DIM)
        qm_ref[:, sl] = _rms(mf[:, sl], mqg_ref[...]).astype(qm_ref.dtype)


def _inproj_prompt(x2d, g1, w_in_b, tables, qg, kg, mqg, pool_w_b, pool_scale, *, batch, seq, tm):
    n = x2d.shape[0]
    tiles_per_seq = seq // tm
    row = lambda i: (i, 0)
    const = lambda i: (0, 0)
    tab = pl.BlockSpec((tm, HEAD_DIM), lambda i: (i % tiles_per_seq, 0))
    head_major = pl.BlockSpec((1, N_HEADS, tm, HEAD_DIM), lambda i: (i // tiles_per_seq, 0, i % tiles_per_seq, 0))
    hm_shape = jax.ShapeDtypeStruct((batch, N_HEADS, seq, HEAD_DIM), F32)
    return pl.pallas_call(
        functools.partial(_inproj_prompt_kernel, tm=tm, tiles_per_seq=tiles_per_seq),
        grid=(n // tm,),
        in_specs=[
            pl.BlockSpec((tm, D_MODEL), row),
            pl.BlockSpec((1, D_MODEL), const),
            pl.BlockSpec((D_MODEL, D_QKV), const, pipeline_mode=pl.Buffered(1)),
            tab, tab, tab,
            pl.BlockSpec((1, HEAD_DIM), const), pl.BlockSpec((1, HEAD_DIM), const), pl.BlockSpec((1, HEAD_DIM), const),
            pl.BlockSpec((len(POOL_WINDOWS), POOL_GROUP_DIM, POOL_GROUP_DIM), lambda i: (0, 0, 0)),
            pl.BlockSpec((1, D_POOL), const),
        ],
        out_specs=[
            pl.BlockSpec((tm, D_POOL), row), head_major, head_major, head_major,
            pl.BlockSpec((tm, D_MEM), row), pl.BlockSpec((tm, D_POOL), row),
        ],
        out_shape=[
            jax.ShapeDtypeStruct((n, D_POOL), F32), hm_shape, hm_shape, hm_shape,
            jax.ShapeDtypeStruct((n, D_MEM), BF16), jax.ShapeDtypeStruct((n, D_POOL), BF16),
        ],
        scratch_shapes=[pltpu.VMEM((tm + POOL_HALO, D_POOL), F32)],
        compiler_params=_params("arbitrary"),
        name="inproj_prompt",
    )(x2d, g1, w_in_b, *tables, qg, kg, mqg, pool_w_b, pool_scale)


def _inproj_sample_kernel(x_ref, g1_ref, w_ref, cos_ref, slo_ref, shi_ref, qg_ref, kg_ref, mqg_ref,
                          pw_ref, pscale_ref, state_ref,
                          u_ref, q_ref, k_ref, v_ref, qm_ref, yp_ref, *, past_len):
    hb = _rms(x_ref[...], g1_ref[...]).astype(BF16)
    u = _dot(hb, w_ref[:, 0:D_POOL])
    u_ref[...] = u
    db = state_ref.shape[0]
    srow = lax.broadcasted_iota(jnp.int32, (db, POOL_STATE, POOL_GROUP_DIM), 1)

    def win_sum(g, w):
        sl = slice(g * POOL_GROUP_DIM, (g + 1) * POOL_GROUP_DIM)
        hist = jnp.where(srow >= POOL_STATE - (w - 1), state_ref[:, :, sl], 0.0)
        return u[:, sl] + jnp.sum(hist, axis=1)

    _pool_branch(u, win_sum, lambda w: float(min(w, past_len + 1)), pw_ref, pscale_ref, yp_ref)

    cos, slo, shi = cos_ref[...], slo_ref[...], shi_ref[...]
    c0 = D_POOL
    qf = _dot(hb, w_ref[:, c0:c0 + D_ATTN])
    c0 += D_ATTN
    kf = _dot(hb, w_ref[:, c0:c0 + D_ATTN])
    for h in range(N_HEADS):
        sl = slice(h * HEAD_DIM, (h + 1) * HEAD_DIM)
        q_ref[:, sl] = _rope(_rms(qf[:, sl], qg_ref[...]), cos, slo, shi)
        k_ref[:, sl] = _rope(_rms(kf[:, sl], kg_ref[...]), cos, slo, shi)
    c0 += D_ATTN
    v_ref[...] = _dot(hb, w_ref[:, c0:c0 + D_ATTN])
    c0 += D_ATTN
    mf = _dot(hb, w_ref[:, c0:c0 + D_MEM])
    for h in range(MEM_HEADS):
        sl = slice(h * HEAD_DIM, (h + 1) * HEAD_DIM)
        qm_ref[:, sl] = _rms(mf[:, sl], mqg_ref[...]).astype(qm_ref.dtype)


def _inproj_sample(x2d, g1, w_in_b, tables, qg, kg, mqg, pool_w_b, pool_scale, state_pool, *, past_len):
    db = x2d.shape[0]
    full = lambda shape: pl.BlockSpec(shape, lambda i: (0,) * len(shape))
    return pl.pallas_call(
        functools.partial(_inproj_sample_kernel, past_len=past_len),
        grid=(1,),
        in_specs=[
            full((db, D_MODEL)), full((1, D_MODEL)),
            pl.BlockSpec((D_MODEL, D_QKV), lambda i: (0, 0), pipeline_mode=pl.Buffered(1)),
            full((db, HEAD_DIM)), full((db, HEAD_DIM)), full((db, HEAD_DIM)),
            full((1, HEAD_DIM)), full((1, HEAD_DIM)), full((1, HEAD_DIM)),
            full((len(POOL_WINDOWS), POOL_GROUP_DIM, POOL_GROUP_DIM)), full((1, D_POOL)),
            full((db, POOL_STATE, D_POOL)),
        ],
        out_specs=[full((db, D_POOL)), full((db, D_ATTN)), full((db, D_ATTN)), full((db, D_ATTN)),
                   full((db, D_MEM)), full((db, D_POOL))],
        out_shape=[jax.ShapeDtypeStruct((db, D_POOL), F32), jax.ShapeDtypeStruct((db, D_ATTN), F32),
                   jax.ShapeDtypeStruct((db, D_ATTN), F32), jax.ShapeDtypeStruct((db, D_ATTN), F32),
                   jax.ShapeDtypeStruct((db, D_MEM), BF16), jax.ShapeDtypeStruct((db, D_POOL), BF16)],
        compiler_params=_params("arbitrary"),
        name="inproj_sample",
    )(x2d, g1, w_in_b, *tables, qg, kg, mqg, pool_w_b, pool_scale, state_pool)


def _moba_prompt_kernel(q_ref, k_ref, v_ref, o_ref,
                        qa_ref, ka_ref, vb_ref, km_ref, gs_ref, bt_ref, tri_ref, s_ref, p_ref, *, seq):
    nb = seq // MOBA_BLOCK
    nbp = -(-nb // SUBLANES) * SUBLANES
    shift = MOBA_BLOCK.bit_length() - 1
    q = q_ref[0, 0]
    k = k_ref[0, 0]
    row_blk = lax.shift_right_logical(lax.broadcasted_iota(jnp.int32, (seq, LANES), 0), shift)
    lane = lax.broadcasted_iota(jnp.int32, (seq, LANES), 1)
    ka_ref[:, 0:HEAD_DIM] = k.astype(BF16)
    ka_ref[:, HEAD_DIM:] = jnp.where(lane == row_blk, 1.0, 0.0).astype(BF16)
    vb_ref[...] = v_ref[0, 0].astype(BF16)
    qa_ref[:, 0:HEAD_DIM] = (q * (SCALE * LOG2E)).astype(BF16)

    km_ref[...] = jnp.zeros(km_ref.shape, F32)
    km_ref[0:nb, :] = jnp.sum(k.reshape(nb, MOBA_BLOCK, HEAD_DIM), axis=1) * (1.0 / MOBA_BLOCK)
    gst = _dot_t(km_ref[...].astype(BF16), q.astype(BF16))
    jidx = lax.broadcasted_iota(jnp.int32, (nbp, seq), 0)
    qblk = lax.shift_right_logical(lax.broadcasted_iota(jnp.int32, (nbp, seq), 1), shift)
    past = jidx < qblk
    gs = jnp.where(past, gst[0:nbp, :], NEG)
    gs_ref[...] = gs
    rank = jnp.zeros((nbp, seq), F32)
    for jp in range(nb):
        other = gs_ref[jp:jp + 1, :]
        rank = rank + jnp.where(jidx > jp, jnp.where(other >= gs, 1.0, 0.0), jnp.where(other > gs, 1.0, 0.0))
    bt_ref[...] = jnp.zeros(bt_ref.shape, F32)
    bt_ref[0:nbp, :] = jnp.where(past, jnp.where(rank < MOBA_TOPK, 0.0, NEG), jnp.where(jidx == qblk, 0.0, NEG))
    for c in range(seq // MOBA_BLOCK):
        rows = slice(c * MOBA_BLOCK, (c + 1) * MOBA_BLOCK)
        qa_ref[rows, HEAD_DIM:] = bt_ref[:, rows].T.astype(BF16)

    tri_ref[...] = jnp.where(lax.broadcasted_iota(jnp.int32, tri_ref.shape, 1)
                             <= lax.broadcasted_iota(jnp.int32, tri_ref.shape, 0), 0.0, NEG)

    for bq in range(nb):
        par = bq % 2
        n_keys = bq + 1
        rows = slice(bq * MOBA_BLOCK, (bq + 1) * MOBA_BLOCK)
        qa = qa_ref[rows, :]
        m_part = None
        for j in range(n_keys):
            s = _dot_t(qa, ka_ref[j * MOBA_BLOCK:(j + 1) * MOBA_BLOCK, :])
            if j == bq:
                s = s + tri_ref[...]
            s_ref[par, j] = s
            mj = jnp.maximum(s[:, :LANES], s[:, LANES:])
            m_part = mj if m_part is None else jnp.maximum(m_part, mj)
        m = jnp.broadcast_to(jnp.max(m_part, axis=-1, keepdims=True), (MOBA_BLOCK, LANES))
        l_part = jnp.zeros((MOBA_BLOCK, LANES), F32)
        for j in range(n_keys):
            for half in range(MOBA_BLOCK // LANES):
                cols = slice(half * LANES, (half + 1) * LANES)
                p = jnp.exp2(s_ref[par, j, :, cols] - m)
                l_part = l_part + p
                p_ref[par, :, j * MOBA_BLOCK + half * LANES:j * MOBA_BLOCK + (half + 1) * LANES] = p.astype(BF16)
        l = jnp.sum(l_part, axis=-1, keepdims=True)
        o = _dot(p_ref[par, :, 0:n_keys * MOBA_BLOCK], vb_ref[0:n_keys * MOBA_BLOCK, :]) / l
        o_ref[rows, :] = o.astype(o_ref.dtype)


def _moba_prompt(q, k, v):
    b, h, s, _ = q.shape
    nb = s // MOBA_BLOCK
    assert nb <= LANES
    nbp = -(-nb // SUBLANES) * SUBLANES
    blk = pl.BlockSpec((1, 1, s, HEAD_DIM), lambda bi, hi: (bi, hi, 0, 0))
    return pl.pallas_call(
        functools.partial(_moba_prompt_kernel, seq=s),
        grid=(b, h),
        in_specs=[blk, blk, blk],
        out_specs=pl.BlockSpec((s, HEAD_DIM), lambda bi, hi: (bi, hi)),
        out_shape=jax.ShapeDtypeStruct((b * s, h * HEAD_DIM), BF16),
        scratch_shapes=[pltpu.VMEM((s, 2 * HEAD_DIM), BF16), pltpu.VMEM((s, 2 * HEAD_DIM), BF16),
                        pltpu.VMEM((s, HEAD_DIM), BF16),
                        pltpu.VMEM((LANES, HEAD_DIM), F32), pltpu.VMEM((nbp, s), F32), pltpu.VMEM((LANES, s), F32),
                        pltpu.VMEM((MOBA_BLOCK, MOBA_BLOCK), F32),
                        pltpu.VMEM((2, nb, MOBA_BLOCK, MOBA_BLOCK), F32), pltpu.VMEM((2, MOBA_BLOCK, s), BF16)],
        compiler_params=_params("arbitrary", "arbitrary"),
        name="moba_prompt",
    )(q, k, v)


def _mem_attend_kernel(q_ref, k_ref, v_ref, o_ref, *, seqs, rows):
    for sq in range(seqs):
        rsl = slice(sq * rows, (sq + 1) * rows)
        for h in range(MEM_HEADS):
            sl = slice(h * HEAD_DIM, (h + 1) * HEAD_DIM)
            qs = (q_ref[rsl, sl].astype(F32) * SCALE).astype(BF16)
            s = _dot_t(qs, k_ref[sq, :, h, :].astype(BF16))
            m = jnp.max(s, axis=-1, keepdims=True)
            p = jnp.exp(s - m)
            l = jnp.sum(p, axis=-1, keepdims=True)
            o = _dot(p.astype(BF16), v_ref[sq, :, h, :].astype(BF16)) / l
            o_ref[rsl, sl] = o.astype(o_ref.dtype)


def _mem_attend(qm, mem_k, mem_v, *, rows_per_seq, tq, seqs_per_step=1):
    n = qm.shape[0]
    m_tok = mem_k.shape[1]
    tiles = rows_per_seq // tq
    assert seqs_per_step == 1 or tiles == 1
    rows = tq * seqs_per_step
    kv = pl.BlockSpec((seqs_per_step, m_tok, MEM_HEADS, HEAD_DIM), lambda i: (i // tiles, 0, 0, 0))
    return pl.pallas_call(
        functools.partial(_mem_attend_kernel, seqs=seqs_per_step, rows=tq),
        grid=(n // rows,),
        in_specs=[pl.BlockSpec((rows, D_MEM), lambda i: (i, 0)), kv, kv],
        out_specs=pl.BlockSpec((rows, D_MEM), lambda i: (i, 0)),
        out_shape=jax.ShapeDtypeStruct((n, D_MEM), BF16),
        compiler_params=_params("arbitrary"),
        name="mem_attend",
    )(qm, mem_k, mem_v)


def _mem_kv_kernel(x_ref, w_ref, g_ref, k_ref, v_ref):
    kv = _dot(x_ref[...].astype(BF16), w_ref[...])
    for h in range(MEM_HEADS):
        sl = slice(h * HEAD_DIM, (h + 1) * HEAD_DIM)
        k_ref[:, sl] = _rms(kv[:, sl], g_ref[...])
    v_ref[...] = kv[:, D_MEM:]


def _mem_kv(mem2d, w_b, g, *, tm):
    n = mem2d.shape[0]
    out = jax.ShapeDtypeStruct((n, D_MEM), F32)
    return pl.pallas_call(
        _mem_kv_kernel,
        grid=(n // tm,),
        in_specs=[pl.BlockSpec((tm, D_MODEL), lambda i: (i, 0)),
                  pl.BlockSpec((D_MODEL, 2 * D_MEM), lambda i: (0, 0)),
                  pl.BlockSpec((1, HEAD_DIM), lambda i: (0, 0))],
        out_specs=[pl.BlockSpec((tm, D_MEM), lambda i: (i, 0))] * 2,
        out_shape=[out, out],
        compiler_params=_params("arbitrary"),
        name="mem_kv",
    )(mem2d, w_b, g)


def _merge_kernel(x_ref, g1_ref, yp_ref, ya_ref, ym_ref, wg0_ref, wg1_ref, wg2_ref,
                  wbp_ref, wba_ref, wbm_ref, wo_ref, o_ref, hb_ref, acc_ref):
    c = pl.program_id(1)

    @pl.when(c == 0)
    def _():
        hb_ref[...] = _rms(x_ref[...], g1_ref[...]).astype(BF16)
        acc_ref[...] = jnp.zeros(acc_ref.shape, F32)

    hb = hb_ref[...]
    merged = (jax.nn.sigmoid(_dot(hb, wg0_ref[...])) * _dot(yp_ref[...].astype(BF16), wbp_ref[...])
              + jax.nn.sigmoid(_dot(hb, wg1_ref[...])) * _dot(ya_ref[...].astype(BF16), wba_ref[...])
              + jax.nn.sigmoid(_dot(hb, wg2_ref[...])) * _dot(ym_ref[...].astype(BF16), wbm_ref[...]))
    acc_ref[...] += _dot(merged.astype(BF16), wo_ref[...])

    @pl.when(c == pl.num_programs(1) - 1)
    def _():
        o_ref[...] = x_ref[...] + acc_ref[...]


def _merge(x2d, g1, yp, ya, ym, w_in_b, wbp_b, wba_b, wbm_b, wo_b, *, tm):
    n = x2d.shape[0]
    tc = MERGE_CHUNK
    n_chunks = D_MODEL // tc
    gate0 = D_QKV // tc
    row = lambda i, c: (i, 0)

    def gate_spec(branch):
        return pl.BlockSpec((D_MODEL, tc), lambda i, c: (0, gate0 + branch * n_chunks + c))

    col = lambda rows: pl.BlockSpec((rows, tc), lambda i, c: (0, c))
    return pl.pallas_call(
        _merge_kernel,
        grid=(n // tm, n_chunks),
        in_specs=[pl.BlockSpec((tm, D_MODEL), row), pl.BlockSpec((1, D_MODEL), lambda i, c: (0, 0)),
                  pl.BlockSpec((tm, D_POOL), row), pl.BlockSpec((tm, D_ATTN), row), pl.BlockSpec((tm, D_MEM), row),
                  gate_spec(0), gate_spec(1), gate_spec(2),
                  col(D_POOL), col(D_ATTN), col(D_MEM),
                  pl.BlockSpec((tc, D_MODEL), lambda i, c: (c, 0))],
        out_specs=pl.BlockSpec((tm, D_MODEL), row),
        out_shape=jax.ShapeDtypeStruct((n, D_MODEL), F32),
        scratch_shapes=[pltpu.VMEM((tm, D_MODEL), BF16), pltpu.VMEM((tm, D_MODEL), F32)],
        compiler_params=_params("arbitrary", "arbitrary"),
        name="merge",
    )(x2d, g1, yp, ya, ym, w_in_b, w_in_b, w_in_b, wbp_b, wba_b, wbm_b, wo_b)


def _ffn_act(ua, ug, a_m1, a_m2, g_m1, g_m2, cwa_ref, cwg_ref, cba_ref, cbg_ref):
    ca = cba_ref[...] + cwa_ref[0:1, :] * a_m2 + cwa_ref[1:2, :] * a_m1 + cwa_ref[2:3, :] * ua
    cg = cbg_ref[...] + cwg_ref[0:1, :] * g_m2 + cwg_ref[1:2, :] * g_m1 + cwg_ref[2:3, :] * ug
    return (jax.nn.silu(ca) * cg).astype(BF16)


def _side_page_copy(pt_ref, cache_ref, buf_ref, sem_ref, group, k, *, n_pages, total_pages, pages_per_step):
    idx = jnp.minimum(group * pages_per_step + k, total_pages - 1)
    page = pt_ref[lax.div(idx, n_pages), lax.rem(idx, n_pages)]
    half = lax.rem(group, 2)
    return pltpu.make_async_copy(cache_ref.at[0, page], buf_ref.at[half, k], sem_ref.at[half, k])


def _ffn_seq_kernel(pt_ref, xu_ref, g2_ref, wua_ref, wug_ref, cwa_ref, cwg_ref, cba_ref, cbg_ref, wd_ref, cache_ref,
                    o_ref, sta_ref, stg_ref, psum_ref,
                    hb_ref, exta_ref, extg_ref, carry_ref, pbuf_ref, psem_ref,
                    *, tm, tiles_per_seq, n_chunks, n_steps, n_pages, total_pages, pages_per_step, n_groups):
    t = pl.program_id(0)
    tu = jnp.minimum(t, n_steps - 1)
    i, c = tu // n_chunks, tu % n_chunks
    n_prev = CONV_WIDTH - 1
    lo = CONV_HALO - n_prev
    page_copy = functools.partial(_side_page_copy, pt_ref, cache_ref, pbuf_ref, psem_ref,
                                  n_pages=n_pages, total_pages=total_pages, pages_per_step=pages_per_step)

    @pl.when(t == 0)
    def _():
        exta_ref[1] = jnp.zeros(exta_ref.shape[1:], F32)
        extg_ref[1] = jnp.zeros(extg_ref.shape[1:], F32)
        pbuf_ref[...] = jnp.zeros(pbuf_ref.shape, F32)
        for k in range(pages_per_step):
            page_copy(0, k).start()

    @pl.when(t + 1 < n_groups)
    def _():
        for k in range(pages_per_step):
            page_copy(t + 1, k).start()

    @pl.when(t < n_groups)
    def _():
        for k in range(pages_per_step):
            page_copy(t, k).wait()

    @pl.when(c == 0)
    def _():
        hb_ref[...] = _rms(xu_ref[...], g2_ref[...]).astype(BF16)

    @pl.when((t == 0) | (t % n_chunks == 1))
    def _():
        o_ref[...] = xu_ref[...]

    @pl.when(i % tiles_per_seq == 0)
    def _():
        carry_ref[c] = jnp.zeros(carry_ref.shape[1:], F32)

    cur, prv = t % 2, (t + 1) % 2
    hb = hb_ref[...]
    ua = _dot(hb, wua_ref[...])
    ug = _dot(hb, wug_ref[...])
    exta_ref[cur, lo:CONV_HALO, :] = carry_ref[c, 0:n_prev, :]
    extg_ref[cur, lo:CONV_HALO, :] = carry_ref[c, n_prev:2 * n_prev, :]
    exta_ref[cur, CONV_HALO:CONV_HALO + tm, :] = ua
    extg_ref[cur, CONV_HALO:CONV_HALO + tm, :] = ug
    a_last = ua[tm - n_prev:tm, :]
    g_last = ug[tm - n_prev:tm, :]
    carry_ref[c, 0:n_prev, :] = a_last
    carry_ref[c, n_prev:2 * n_prev, :] = g_last

    rows = lambda ref, back: ref[prv, CONV_HALO - back:CONV_HALO - back + tm, :]
    act = _ffn_act(rows(exta_ref, 0), rows(extg_ref, 0), rows(exta_ref, 1), rows(exta_ref, 2),
                   rows(extg_ref, 1), rows(extg_ref, 2), cwa_ref, cwg_ref, cba_ref, cbg_ref)
    o_ref[...] += _dot(act, wd_ref[...])

    for k in range(pages_per_step):
        psum_ref[k] = jnp.sum(pbuf_ref[cur, k], axis=1)

    @pl.when(i % tiles_per_seq == tiles_per_seq - 1)
    def _():
        sta_ref[i // tiles_per_seq, c] = a_last
        stg_ref[i // tiles_per_seq, c] = g_last


def _ffn_step_kernel(x_ref, g2_ref, wua_ref, wug_ref, cwa_ref, cwg_ref, cba_ref, cbg_ref, wd_ref,
                     am2_ref, gm2_ref, am1_ref, gm1_ref,
                     o_ref, ua_ref, ug_ref, hb_ref, acc_ref):
    c = pl.program_id(1)

    @pl.when(c == 0)
    def _():
        hb_ref[...] = _rms(x_ref[...], g2_ref[...]).astype(BF16)
        acc_ref[...] = jnp.zeros(acc_ref.shape, F32)

    hb = hb_ref[...]
    ua = _dot(hb, wua_ref[...])
    ug = _dot(hb, wug_ref[...])
    ua_ref[...] = ua
    ug_ref[...] = ug
    act = _ffn_act(ua, ug, am1_ref[...], am2_ref[...], gm1_ref[...], gm2_ref[...], cwa_ref, cwg_ref, cba_ref, cbg_ref)
    acc_ref[...] += _dot(act, wd_ref[...])

    @pl.when(c == pl.num_programs(1) - 1)
    def _():
        o_ref[...] = x_ref[...] + acc_ref[...]


def _ffn_seq(x2d, g2, w_up_b, conv_w, conv_b2, w_down_b, page_table, cache_k, *, tm, batch, seq):
    n = x2d.shape[0]
    tc = FFN_CHUNK
    n_chunks = D_FF // tc
    assert n_chunks >= 2
    tiles_per_seq = seq // tm
    n_prev = CONV_WIDTH - 1
    n_steps = (n // tm) * n_chunks
    db, n_pages = page_table.shape
    total_pages = db * n_pages
    pages_per_step = -(-total_pages // n_steps)
    n_groups = -(-total_pages // pages_per_step)
    up_tile = lambda t: jnp.minimum(t, n_steps - 1) // n_chunks
    up_chunk = lambda t: jnp.minimum(t, n_steps - 1) % n_chunks
    down_tile = lambda t: jnp.maximum(t - 1, 0) // n_chunks
    down_chunk = lambda t: jnp.maximum(t - 1, 0) % n_chunks
    a_up = lambda rows: pl.BlockSpec((rows, tc), lambda t, pt: (0, up_chunk(t)))
    g_up = lambda rows: pl.BlockSpec((rows, tc), lambda t, pt: (0, n_chunks + up_chunk(t)))
    a_dn = lambda rows: pl.BlockSpec((rows, tc), lambda t, pt: (0, down_chunk(t)))
    g_dn = lambda rows: pl.BlockSpec((rows, tc), lambda t, pt: (0, n_chunks + down_chunk(t)))
    st_spec = pl.BlockSpec((batch, n_chunks, n_prev, tc), lambda t, pt: (0, 0, 0, 0))
    st_shape = jax.ShapeDtypeStruct((batch, n_chunks, n_prev, tc), F32)
    ps_spec = pl.BlockSpec((pages_per_step, N_HEADS, HEAD_DIM), lambda t, pt: (jnp.minimum(t, n_groups), 0, 0))
    ps_shape = jax.ShapeDtypeStruct(((n_groups + 1) * pages_per_step, N_HEADS, HEAD_DIM), F32)
    y, sta, stg, psums = pl.pallas_call(
        functools.partial(_ffn_seq_kernel, tm=tm, tiles_per_seq=tiles_per_seq, n_chunks=n_chunks, n_steps=n_steps,
                          n_pages=n_pages, total_pages=total_pages, pages_per_step=pages_per_step, n_groups=n_groups),
        grid_spec=pltpu.PrefetchScalarGridSpec(
            num_scalar_prefetch=1,
            grid=(n_steps + 1,),
            in_specs=[pl.BlockSpec((tm, D_MODEL), lambda t, pt: (up_tile(t), 0)),
                      pl.BlockSpec((1, D_MODEL), lambda t, pt: (0, 0)),
                      a_up(D_MODEL), g_up(D_MODEL), a_dn(CONV_WIDTH), g_dn(CONV_WIDTH), a_dn(1), g_dn(1),
                      pl.BlockSpec((tc, D_MODEL), lambda t, pt: (down_chunk(t), 0)),
                      pl.BlockSpec(memory_space=pl.ANY)],
            out_specs=[pl.BlockSpec((tm, D_MODEL), lambda t, pt: (down_tile(t), 0)), st_spec, st_spec, ps_spec],
            scratch_shapes=[pltpu.VMEM((tm, D_MODEL), BF16),
                            pltpu.VMEM((2, tm + CONV_HALO, tc), F32), pltpu.VMEM((2, tm + CONV_HALO, tc), F32),
                            pltpu.VMEM((n_chunks, SUBLANES, tc), F32),
                            pltpu.VMEM((2, pages_per_step, N_HEADS, PAGE_SIZE, HEAD_DIM), F32),
                            pltpu.SemaphoreType.DMA((2, pages_per_step))],
        ),
        out_shape=[jax.ShapeDtypeStruct((n, D_MODEL), F32), st_shape, st_shape, ps_shape],
        compiler_params=_params("arbitrary"),
        name="ffn_seq",
    )(page_table, x2d, g2, w_up_b, w_up_b, conv_w, conv_w, conv_b2, conv_b2, w_down_b, cache_k)
    unchunk = lambda st: st.transpose(0, 2, 1, 3).reshape(batch, n_prev, D_FF)
    page_sums = psums[:total_pages].reshape(db, n_pages, N_HEADS, HEAD_DIM).transpose(0, 2, 1, 3)
    return y, unchunk(sta), unchunk(stg), page_sums


def _ffn_step(x2d, g2, w_up_b, conv_w, conv_b2, w_down_b, prev2, prev1, *, tm):
    n = x2d.shape[0]
    tc = FFN_CHUNK
    n_chunks = D_FF // tc
    row = lambda i, c: (i, 0)
    a_col = lambda rows: pl.BlockSpec((rows, tc), lambda i, c: (0, c))
    g_col = lambda rows: pl.BlockSpec((rows, tc), lambda i, c: (0, n_chunks + c))
    rows_a = pl.BlockSpec((tm, tc), lambda i, c: (i, c))
    rows_g = pl.BlockSpec((tm, tc), lambda i, c: (i, n_chunks + c))
    up_shape = jax.ShapeDtypeStruct((n, D_FF), F32)
    return pl.pallas_call(
        _ffn_step_kernel,
        grid=(n // tm, n_chunks),
        in_specs=[pl.BlockSpec((tm, D_MODEL), row), pl.BlockSpec((1, D_MODEL), lambda i, c: (0, 0)),
                  a_col(D_MODEL), g_col(D_MODEL), a_col(CONV_WIDTH), g_col(CONV_WIDTH), a_col(1), g_col(1),
                  pl.BlockSpec((tc, D_MODEL), lambda i, c: (c, 0)),
                  rows_a, rows_g, rows_a, rows_g],
        out_specs=[pl.BlockSpec((tm, D_MODEL), row), rows_a, rows_a],
        out_shape=[jax.ShapeDtypeStruct((n, D_MODEL), F32), up_shape, up_shape],
        scratch_shapes=[pltpu.VMEM((tm, D_MODEL), BF16), pltpu.VMEM((tm, D_MODEL), F32)],
        compiler_params=_params("arbitrary", "arbitrary"),
        name="ffn_step",
    )(x2d, g2, w_up_b, w_up_b, conv_w, conv_w, conv_b2, conv_b2, w_down_b, prev2, prev2, prev1, prev1)


def _block_gate_kernel(q_ref, ps_ref, o_ref, km_ref, *, n_blocks, seqs):
    km_ref[...] = jnp.zeros(km_ref.shape, F32)
    lane = lax.broadcasted_iota(jnp.int32, (SUBLANES, LANES), 1)

    def one_seq(sq, carry):
        for h in range(N_HEADS):
            even = ps_ref[sq, h, pl.ds(0, n_blocks, stride=PAGES_PER_BLOCK), :]
            odd = ps_ref[sq, h, pl.ds(1, n_blocks, stride=PAGES_PER_BLOCK), :]
            km_ref[0:n_blocks, :] = (even + odd) * (1.0 / MOBA_BLOCK)
            qh = jnp.broadcast_to(q_ref[sq, :, h * HEAD_DIM:(h + 1) * HEAD_DIM], (SUBLANES, HEAD_DIM))
            gs = lax.dot_general(qh, km_ref[...], (((1,), (1,)), ((), ())), preferred_element_type=F32,
                                 precision=lax.Precision.HIGHEST)
            gs = jnp.where(lane < n_blocks, gs, NEG)
            i0, i1, i2 = _top3(gs, n_blocks)
            picked = jnp.where(lane == 0, i0, jnp.where(lane == 1, i1, jnp.where(lane == 2, i2, 0)))
            o_ref[sq, h:h + 1, :] = picked[0:1, :]
        return carry

    lax.fori_loop(0, seqs, one_seq, 0)


def _block_gate(q3, page_sums):
    db, _, n_pages, _ = page_sums.shape
    n_blocks = n_pages // PAGES_PER_BLOCK
    assert PAGES_PER_BLOCK == 2 and MOBA_TOPK <= n_blocks <= LANES
    seqs = _row_tile(db, SUBLANES)
    return pl.pallas_call(
        functools.partial(_block_gate_kernel, n_blocks=n_blocks, seqs=seqs),
        grid=(db // seqs,),
        in_specs=[pl.BlockSpec((seqs, 1, D_ATTN), lambda b: (b, 0, 0)),
                  pl.BlockSpec((seqs, N_HEADS, n_pages, HEAD_DIM), lambda b: (b, 0, 0, 0))],
        out_specs=pl.BlockSpec((seqs, N_HEADS, LANES), lambda b: (b, 0, 0)),
        out_shape=jax.ShapeDtypeStruct((db, N_HEADS, LANES), jnp.int32),
        scratch_shapes=[pltpu.VMEM((LANES, HEAD_DIM), F32)],
        compiler_params=_params("arbitrary"),
        name="block_gate",
    )(q3, page_sums)


N_SEL_PAGES = MOBA_TOPK * PAGES_PER_BLOCK


def _sel_copy(pt_ref, sel_ref, cache_ref, buf_ref, sem_ref, b, h, s):
    blk = sel_ref[b, h * MOBA_TOPK + s // PAGES_PER_BLOCK]
    page = pt_ref[b, blk * PAGES_PER_BLOCK + s % PAGES_PER_BLOCK]
    half = lax.rem(b, 2)
    return pltpu.make_async_copy(cache_ref.at[0, page, h], buf_ref.at[half, h, s], sem_ref.at[half, h])


def _moba_sample_kernel(pt_ref, sel_ref, q_ref, kn_ref, vn_ref, ck_ref, cv_ref, o_ref,
                        kbuf_ref, vbuf_ref, ksem_ref, vsem_ref):
    b = pl.program_id(0)

    def fetch(seq):
        for h in range(N_HEADS):
            for s in range(N_SEL_PAGES):
                _sel_copy(pt_ref, sel_ref, ck_ref, kbuf_ref, ksem_ref, seq, h, s).start()
                _sel_copy(pt_ref, sel_ref, cv_ref, vbuf_ref, vsem_ref, seq, h, s).start()

    @pl.when(b == 0)
    def _():
        fetch(b)

    @pl.when(b + 1 < pl.num_programs(0))
    def _():
        fetch(b + 1)

    half = lax.rem(b, 2)
    for h in range(N_HEADS):
        for s in range(N_SEL_PAGES):
            _sel_copy(pt_ref, sel_ref, ck_ref, kbuf_ref, ksem_ref, b, h, s).wait()
            _sel_copy(pt_ref, sel_ref, cv_ref, vbuf_ref, vsem_ref, b, h, s).wait()
        sl = slice(h * HEAD_DIM, (h + 1) * HEAD_DIM)
        qs = q_ref[0, :, sl] * SCALE
        q8 = jnp.broadcast_to(qs, (SUBLANES, HEAD_DIM)).astype(BF16)
        kk = kbuf_ref[half, h].reshape(N_SEL_PAGES * PAGE_SIZE, HEAD_DIM).astype(BF16)
        vv = vbuf_ref[half, h].reshape(N_SEL_PAGES * PAGE_SIZE, HEAD_DIM).astype(BF16)
        s_sel = _dot_t(q8, kk)
        s_own = jnp.sum(qs * kn_ref[0, :, sl], axis=-1, keepdims=True)
        m = jnp.maximum(jnp.max(s_sel, axis=-1, keepdims=True), s_own)
        p = jnp.exp(s_sel - m)
        p_own = jnp.exp(s_own - m)
        l = jnp.sum(p, axis=-1, keepdims=True) + p_own
        o = (_dot(p.astype(BF16), vv) + p_own * vn_ref[0, :, sl]) / l
        o_ref[0, :, sl] = o[0:1, :].astype(o_ref.dtype)


def _moba_sample(page_table, sel, q3, k3, v3, cache_k, cache_v):
    db = q3.shape[0]
    tok = pl.BlockSpec((1, 1, D_ATTN), lambda b, pt, sl: (b, 0, 0))
    return pl.pallas_call(
        _moba_sample_kernel,
        grid_spec=pltpu.PrefetchScalarGridSpec(
            num_scalar_prefetch=2,
            grid=(db,),
            in_specs=[tok, tok, tok, pl.BlockSpec(memory_space=pl.ANY), pl.BlockSpec(memory_space=pl.ANY)],
            out_specs=tok,
            scratch_shapes=[pltpu.VMEM((2, N_HEADS, N_SEL_PAGES, PAGE_SIZE, HEAD_DIM), F32),
                            pltpu.VMEM((2, N_HEADS, N_SEL_PAGES, PAGE_SIZE, HEAD_DIM), F32),
                            pltpu.SemaphoreType.DMA((2, N_HEADS)), pltpu.SemaphoreType.DMA((2, N_HEADS))],
        ),
        out_shape=jax.ShapeDtypeStruct((db, 1, D_ATTN), F32),
        compiler_params=_params("arbitrary"),
        name="moba_sample",
    )(page_table, sel, q3, k3, v3, cache_k, cache_v)


def _row_tile(n, pref):
    return pref if n % pref == 0 else n


def kernel(x_prompt, x_sample, mem_prompt, cache_k, cache_v, page_table, cache_mem_k, cache_mem_v, state_pool, state_conv, norm1_g, w_in, q_norm_g, k_norm_g, mem_q_norm_g, mem_k_norm_g, w_mem_kv, pool_w, pool_scale, w_branch_pool, w_branch_attn, w_branch_mem, w_out, norm2_g, w_up, conv_w, conv_b, w_down):
    b, s, _ = x_prompt.shape
    db, t, _ = x_sample.shape
    assert t == 1 and norm1_g.shape[0] == 1 and s % MOBA_BLOCK == 0
    n_pages = page_table.shape[1]
    assert n_pages % PAGES_PER_BLOCK == 0
    m_tok = mem_prompt.shape[1]

    w_in_b = w_in[0].astype(BF16)
    wbp_b, wba_b, wbm_b = (w[0].astype(BF16) for w in (w_branch_pool, w_branch_attn, w_branch_mem))
    wo_b, w_up_b, w_down_b = w_out[0].astype(BF16), w_up[0].astype(BF16), w_down[0].astype(BF16)
    w_mem_b, pool_w_b = w_mem_kv[0].astype(BF16), pool_w[0].astype(BF16)
    g1, g2 = norm1_g, norm2_g
    conv_b2 = conv_b
    norm_args = (q_norm_g, k_norm_g, mem_q_norm_g, pool_w_b, pool_scale)

    n = b * s
    x2d = x_prompt.reshape(n, D_MODEL)
    tm = _row_tile(s, 512)
    tables_p = _rope_tables(jnp.arange(s, dtype=jnp.int32))
    u_p, q_p, k_p, v_p, qm_p, yp_p = _inproj_prompt(x2d, g1, w_in_b, tables_p, *norm_args, batch=b, seq=s, tm=tm)
    mk2d, mv2d = _mem_kv(mem_prompt.reshape(b * m_tok, D_MODEL), w_mem_b, mem_k_norm_g, tm=m_tok)
    ya_p = _moba_prompt(q_p, k_p, v_p)
    mk_p = mk2d.reshape(b, m_tok, MEM_HEADS, HEAD_DIM)
    mv_p = mv2d.reshape(b, m_tok, MEM_HEADS, HEAD_DIM)
    ym_p = _mem_attend(qm_p, mk_p, mv_p, rows_per_seq=s, tq=_row_tile(s, 1024))
    x1_p = _merge(x2d, g1, yp_p, ya_p, ym_p, w_in_b, wbp_b, wba_b, wbm_b, wo_b, tm=tm)
    y_p, sta_p, stg_p, sums = _ffn_seq(x1_p, g2, w_up_b, conv_w[0], conv_b2, w_down_b, page_table, cache_k,
                                       tm=tm, batch=b, seq=s)

    xs2d = x_sample.reshape(db, D_MODEL)
    past_len = n_pages * PAGE_SIZE
    pos_s = jnp.full((db,), past_len, jnp.int32)
    u_s, q_s, k_s, v_s, qm_s, yp_s = _inproj_sample(xs2d, g1, w_in_b, _rope_tables(pos_s), *norm_args, state_pool[0],
                                                    past_len=past_len)
    q3, k3, v3 = (a.reshape(db, 1, D_ATTN) for a in (q_s, k_s, v_s))
    sel = _block_gate(q3, sums)[:, :, :MOBA_TOPK].reshape(db, N_HEADS * MOBA_TOPK)
    ya_s = _moba_sample(page_table, sel, q3, k3, v3, cache_k, cache_v).reshape(db, D_ATTN)
    qm_rep = jnp.broadcast_to(qm_s[:, None, :], (db, BF16_ROWS, D_MEM)).reshape(db * BF16_ROWS, D_MEM)
    ym_s = _mem_attend(qm_rep, cache_mem_k[0], cache_mem_v[0], rows_per_seq=BF16_ROWS, tq=BF16_ROWS,
                       seqs_per_step=_row_tile(db, SUBLANES)).reshape(db, BF16_ROWS, D_MEM)[:, 0]
    x1_s = _merge(xs2d, g1, yp_s, ya_s, ym_s, w_in_b, wbp_b, wba_b, wbm_b, wo_b, tm=db)
    y_s, ua_s, ug_s = _ffn_step(x1_s, g2, w_up_b, conv_w[0], conv_b2, w_down_b,
                                state_conv[0, :, 0], state_conv[0, :, 1], tm=db)

    up_s = jnp.concatenate([ua_s, ug_s], axis=-1)
    return (
        y_p.reshape(b, s, D_MODEL),
        y_s.reshape(db, 1, D_MODEL),
        k_p[None], v_p[None],
        k_s.reshape(1, db, N_HEADS, 1, HEAD_DIM), v_s.reshape(1, db, N_HEADS, 1, HEAD_DIM),
        mk_p[None], mv_p[None],
        u_p.reshape(b, s, D_POOL)[None, :, s - POOL_STATE:],
        jnp.concatenate([state_pool[0][:, 1:], u_s[:, None]], axis=1)[None],
        jnp.concatenate([sta_p, stg_p], axis=-1)[None],
        jnp.concatenate([state_conv[0][:, 1:], up_s[:, None]], axis=1)[None],
    )
```

```python
import functools

import jax
import jax.numpy as jnp
import numpy as np
from jax import lax
from jax.experimental import pallas as pl
from jax.experimental.pallas import tpu as pltpu

F32 = jnp.float32
BF16 = jnp.bfloat16

D_MODEL = 2048
HEAD_DIM = 128
N_HEADS = 8
MEM_HEADS = 4
POOL_WINDOWS = (2, 4, 8, 16)
POOL_GROUP_DIM = 128
D_POOL = len(POOL_WINDOWS) * POOL_GROUP_DIM
POOL_STATE = max(POOL_WINDOWS) - 1
D_ATTN = N_HEADS * HEAD_DIM
D_MEM = MEM_HEADS * HEAD_DIM
D_QKV = D_POOL + 3 * D_ATTN + D_MEM
N_BRANCH = 3
MOBA_BLOCK = 256
MOBA_TOPK = 3
PAGE_SIZE = 128
PAGES_PER_BLOCK = MOBA_BLOCK // PAGE_SIZE
ROPE_THETA = 500000.0
ROPE_DIM = HEAD_DIM // 4
ROPE_HALF = ROPE_DIM // 2
D_FF = 5632
CONV_WIDTH = 3
EPS = 1e-6
NEG = -1e30
SCALE = HEAD_DIM ** -0.5
LOG2E = 1.4426950408889634

LANES = 128
SUBLANES = 8
BF16_ROWS = 2 * SUBLANES
POOL_HALO = 16
CONV_HALO = SUBLANES
VMEM_LIMIT = 56 * 1024 * 1024
FFN_CHUNK = 512
MERGE_CHUNK = 512
PAGE_DMA_PRIORITY = 1


def _params(*sem):
    return pltpu.CompilerParams(dimension_semantics=sem, vmem_limit_bytes=VMEM_LIMIT)


def _rms(x, g):
    ms = jnp.mean(x * x, axis=-1, keepdims=True)
    return x * lax.rsqrt(ms + EPS) * g


def _rope(t, cos, sin_lo, sin_hi):
    return (t * cos + pltpu.roll(t, ROPE_HALF, 1) * sin_hi
            + pltpu.roll(t, HEAD_DIM - ROPE_HALF, 1) * sin_lo)


def _rope_tables(pos):
    half = ROPE_HALF
    inv = ROPE_THETA ** (-jnp.arange(half, dtype=F32) * (2.0 / ROPE_DIM))
    ang = pos.astype(F32)[:, None] * inv[None, :]
    cos, sin = jnp.cos(ang), jnp.sin(ang)
    n = pos.shape[0]
    pad = jnp.zeros((n, HEAD_DIM - ROPE_DIM), F32)
    zero = jnp.zeros((n, half), F32)
    cos_t = jnp.concatenate([cos, cos, pad + 1.0], axis=1)
    sin_lo = jnp.concatenate([-sin, zero, pad], axis=1)
    sin_hi = jnp.concatenate([zero, sin, pad], axis=1)
    return cos_t, sin_lo, sin_hi


def _dot(a, b):
    return jnp.dot(a, b, preferred_element_type=F32)


def _dot_t(a, b):
    return lax.dot_general(a, b, (((1,), (1,)), ((), ())), preferred_element_type=F32)


def _top3(gs, n_valid):
    lane = lax.broadcasted_iota(jnp.int32, gs.shape, 1)
    out = []
    for r in range(MOBA_TOPK):
        mx = jnp.max(gs, axis=-1, keepdims=True)
        idx = jnp.min(jnp.where(gs == mx, lane, LANES), axis=-1, keepdims=True)
        out.append(jnp.where(r < n_valid, idx, -1))
        gs = jnp.where(lane == idx, -jnp.inf, gs)
    return out


def _pool_branch(u, win_sum, cnt, pw_ref, pscale_ref, yp_ref):
    for g, w in enumerate(POOL_WINDOWS):
        sl = slice(g * POOL_GROUP_DIM, (g + 1) * POOL_GROUP_DIM)
        p = win_sum(g, w) / cnt(w) - u[:, sl]
        y = _dot(p.astype(BF16), pw_ref[g]) * pscale_ref[:, sl]
        yp_ref[:, sl] = y.astype(yp_ref.dtype)


def _inproj_prompt_kernel(x_ref, g1_ref, w_ref, cos_ref, slo_ref, shi_ref, qg_ref, kg_ref, mqg_ref,
                          pw_ref, pscale_ref,
                          u_ref, q_ref, k_ref, v_ref, qm_ref, yp_ref,
                          uext_ref, *, tm, tiles_per_seq):
    i = pl.program_id(0)
    t_in_seq = i % tiles_per_seq
    hb = _rms(x_ref[...], g1_ref[...]).astype(BF16)

    u = _dot(hb, w_ref[:, 0:D_POOL])
    u_ref[...] = u

    @pl.when(t_in_seq == 0)
    def _():
        uext_ref[0:POOL_HALO, :] = jnp.zeros((POOL_HALO, D_POOL), F32)

    uext_ref[POOL_HALO:POOL_HALO + tm, :] = u
    pos = t_in_seq * tm + lax.broadcasted_iota(jnp.int32, (tm, 1), 0)

    def win_sum(g, w):
        sl = slice(g * POOL_GROUP_DIM, (g + 1) * POOL_GROUP_DIM)
        s = uext_ref[POOL_HALO:POOL_HALO + tm, sl]
        for d in range(1, w):
            s = s + uext_ref[POOL_HALO - d:POOL_HALO - d + tm, sl]
        return s

    cos, slo, shi = cos_ref[...], slo_ref[...], shi_ref[...]
    c0 = D_POOL
    qf = _dot(hb, w_ref[:, c0:c0 + D_ATTN])
    for h in range(N_HEADS):
        t = _rms(qf[:, h * HEAD_DIM:(h + 1) * HEAD_DIM], qg_ref[...])
        q_ref[0, h] = _rope(t, cos, slo, shi)
    c0 += D_ATTN
    kf = _dot(hb, w_ref[:, c0:c0 + D_ATTN])
    for h in range(N_HEADS):
        t = _rms(kf[:, h * HEAD_DIM:(h + 1) * HEAD_DIM], kg_ref[...])
        k_ref[0, h] = _rope(t, cos, slo, shi)
    c0 += D_ATTN
    vf = _dot(hb, w_ref[:, c0:c0 + D_ATTN])
    for h in range(N_HEADS):
        v_ref[0, h] = vf[:, h * HEAD_DIM:(h + 1) * HEAD_DIM]
    c0 += D_ATTN
    mf = _dot(hb, w_ref[:, c0:c0 + D_MEM])
    for h in range(MEM_HEADS):
        sl = slice(h * HEAD_DIM, (h + 1) * HEAD_DIM)
        qm_ref[:, sl] = _rms(mf[:, sl], mqg_ref[...]).astype(qm_ref.dtype)

    _pool_branch(u, win_sum, lambda w: jnp.minimum(w, pos + 1).astype(F32), pw_ref, pscale_ref, yp_ref)
    uext_ref[0:POOL_HALO, :] = uext_ref[tm:tm + POOL_HALO, :]


def _inproj_prompt(x2d, g1, w_in_b, tables, qg, kg, mqg, pool_w_b, pool_scale, *, batch, seq, tm):
    n = x2d.shape[0]
    tiles_per_seq = seq // tm
    row = lambda i: (i, 0)
    const = lambda i: (0, 0)
    tab = pl.BlockSpec((tm, HEAD_DIM), lambda i: (i % tiles_per_seq, 0))
    head_major = pl.BlockSpec((1, N_HEADS, tm, HEAD_DIM), lambda i: (i // tiles_per_seq, 0, i % tiles_per_seq, 0))
    hm_shape = jax.ShapeDtypeStruct((batch, N_HEADS, seq, HEAD_DIM), F32)
    return pl.pallas_call(
        functools.partial(_inproj_prompt_kernel, tm=tm, tiles_per_seq=tiles_per_seq),
        grid=(n // tm,),
        in_specs=[
            pl.BlockSpec((tm, D_MODEL), row),
            pl.BlockSpec((1, D_MODEL), const),
            pl.BlockSpec((D_MODEL, D_QKV), const, pipeline_mode=pl.Buffered(1)),
            tab, tab, tab,
            pl.BlockSpec((1, HEAD_DIM), const), pl.BlockSpec((1, HEAD_DIM), const), pl.BlockSpec((1, HEAD_DIM), const),
            pl.BlockSpec((len(POOL_WINDOWS), POOL_GROUP_DIM, POOL_GROUP_DIM), lambda i: (0, 0, 0)),
            pl.BlockSpec((1, D_POOL), const),
        ],
        out_specs=[
            pl.BlockSpec((tm, D_POOL), row), head_major, head_major, head_major,
            pl.BlockSpec((tm, D_MEM), row), pl.BlockSpec((tm, D_POOL), row),
        ],
        out_shape=[
            jax.ShapeDtypeStruct((n, D_POOL), F32), hm_shape, hm_shape, hm_shape,
            jax.ShapeDtypeStruct((n, D_MEM), BF16), jax.ShapeDtypeStruct((n, D_POOL), BF16),
        ],
        scratch_shapes=[pltpu.VMEM((tm + POOL_HALO, D_POOL), F32)],
        compiler_params=_params("arbitrary"),
        name="inproj_prompt",
    )(x2d, g1, w_in_b, *tables, qg, kg, mqg, pool_w_b, pool_scale)


def _inproj_sample_kernel(x_ref, g1_ref, w_ref, cos_ref, slo_ref, shi_ref, qg_ref, kg_ref, mqg_ref,
                          pw_ref, pscale_ref, state_ref,
                          u_ref, q_ref, k_ref, v_ref, qm_ref, yp_ref, *, past_len):
    hb = _rms(x_ref[...], g1_ref[...]).astype(BF16)
    u = _dot(hb, w_ref[:, 0:D_POOL])
    u_ref[...] = u
    db = state_ref.shape[0]
    srow = lax.broadcasted_iota(jnp.int32, (db, POOL_STATE, POOL_GROUP_DIM), 1)

    def win_sum(g, w):
        sl = slice(g * POOL_GROUP_DIM, (g + 1) * POOL_GROUP_DIM)
        hist = jnp.where(srow >= POOL_STATE - (w - 1), state_ref[:, :, sl], 0.0)
        return u[:, sl] + jnp.sum(hist, axis=1)

    _pool_branch(u, win_sum, lambda w: float(min(w, past_len + 1)), pw_ref, pscale_ref, yp_ref)

    cos, slo, shi = cos_ref[...], slo_ref[...], shi_ref[...]
    c0 = D_POOL
    qf = _dot(hb, w_ref[:, c0:c0 + D_ATTN])
    c0 += D_ATTN
    kf = _dot(hb, w_ref[:, c0:c0 + D_ATTN])
    for h in range(N_HEADS):
        sl = slice(h * HEAD_DIM, (h + 1) * HEAD_DIM)
        q_ref[:, sl] = _rope(_rms(qf[:, sl], qg_ref[...]), cos, slo, shi)
        k_ref[:, sl] = _rope(_rms(kf[:, sl], kg_ref[...]), cos, slo, shi)
    c0 += D_ATTN
    v_ref[...] = _dot(hb, w_ref[:, c0:c0 + D_ATTN])
    c0 += D_ATTN
    mf = _dot(hb, w_ref[:, c0:c0 + D_MEM])
    for h in range(MEM_HEADS):
        sl = slice(h * HEAD_DIM, (h + 1) * HEAD_DIM)
        qm_ref[:, sl] = _rms(mf[:, sl], mqg_ref[...]).astype(qm_ref.dtype)


def _inproj_sample(x2d, g1, w_in_b, tables, qg, kg, mqg, pool_w_b, pool_scale, state_pool, *, past_len):
    db = x2d.shape[0]
    full = lambda shape: pl.BlockSpec(shape, lambda i: (0,) * len(shape))
    return pl.pallas_call(
        functools.partial(_inproj_sample_kernel, past_len=past_len),
        grid=(1,),
        in_specs=[
            full((db, D_MODEL)), full((1, D_MODEL)),
            pl.BlockSpec((D_MODEL, D_QKV), lambda i: (0, 0), pipeline_mode=pl.Buffered(1)),
            full((db, HEAD_DIM)), full((db, HEAD_DIM)), full((db, HEAD_DIM)),
            full((1, HEAD_DIM)), full((1, HEAD_DIM)), full((1, HEAD_DIM)),
            full((len(POOL_WINDOWS), POOL_GROUP_DIM, POOL_GROUP_DIM)), full((1, D_POOL)),
            full((db, POOL_STATE, D_POOL)),
        ],
        out_specs=[full((db, D_POOL)), full((db, D_ATTN)), full((db, D_ATTN)), full((db, D_ATTN)),
                   full((db, D_MEM)), full((db, D_POOL))],
        out_shape=[jax.ShapeDtypeStruct((db, D_POOL), F32), jax.ShapeDtypeStruct((db, D_ATTN), F32),
                   jax.ShapeDtypeStruct((db, D_ATTN), F32), jax.ShapeDtypeStruct((db, D_ATTN), F32),
                   jax.ShapeDtypeStruct((db, D_MEM), BF16), jax.ShapeDtypeStruct((db, D_POOL), BF16)],
        compiler_params=_params("arbitrary"),
        name="inproj_sample",
    )(x2d, g1, w_in_b, *tables, qg, kg, mqg, pool_w_b, pool_scale, state_pool)


def _moba_prompt_kernel(q_ref, k_ref, v_ref, o_ref,
                        qa_ref, ka_ref, vb_ref, km_ref, gs_ref, bt_ref, tri_ref, s_ref, p_ref, *, seq):
    nb = seq // MOBA_BLOCK
    nbp = -(-nb // SUBLANES) * SUBLANES
    shift = MOBA_BLOCK.bit_length() - 1
    q = q_ref[0, 0]
    k = k_ref[0, 0]
    row_blk = lax.shift_right_logical(lax.broadcasted_iota(jnp.int32, (seq, LANES), 0), shift)
    lane = lax.broadcasted_iota(jnp.int32, (seq, LANES), 1)
    ka_ref[:, 0:HEAD_DIM] = k.astype(BF16)
    ka_ref[:, HEAD_DIM:] = jnp.where(lane == row_blk, 1.0, 0.0).astype(BF16)
    vb_ref[...] = v_ref[0, 0].astype(BF16)
    qa_ref[:, 0:HEAD_DIM] = (q * (SCALE * LOG2E)).astype(BF16)

    km_ref[...] = jnp.zeros(km_ref.shape, F32)
    km_ref[0:nb, :] = jnp.sum(k.reshape(nb, MOBA_BLOCK, HEAD_DIM), axis=1) * (1.0 / MOBA_BLOCK)
    gst = _dot_t(km_ref[...].astype(BF16), q.astype(BF16))
    jidx = lax.broadcasted_iota(jnp.int32, (nbp, seq), 0)
    qblk = lax.shift_right_logical(lax.broadcasted_iota(jnp.int32, (nbp, seq), 1), shift)
    past = jidx < qblk
    gs = jnp.where(past, gst[0:nbp, :], NEG)
    gs_ref[...] = gs
    rank = jnp.zeros((nbp, seq), F32)
    for jp in range(nb):
        other = gs_ref[jp:jp + 1, :]
        rank = rank + jnp.where(jidx > jp, jnp.where(other >= gs, 1.0, 0.0), jnp.where(other > gs, 1.0, 0.0))
    bt_ref[...] = jnp.zeros(bt_ref.shape, F32)
    bt_ref[0:nbp, :] = jnp.where(past, jnp.where(rank < MOBA_TOPK, 0.0, NEG), jnp.where(jidx == qblk, 0.0, NEG))
    for c in range(seq // MOBA_BLOCK):
        rows = slice(c * MOBA_BLOCK, (c + 1) * MOBA_BLOCK)
        qa_ref[rows, HEAD_DIM:] = bt_ref[:, rows].T.astype(BF16)

    tri_ref[...] = jnp.where(lax.broadcasted_iota(jnp.int32, tri_ref.shape, 1)
                             <= lax.broadcasted_iota(jnp.int32, tri_ref.shape, 0), 0.0, NEG)

    for bq in range(nb):
        par = bq % 2
        n_keys = bq + 1
        rows = slice(bq * MOBA_BLOCK, (bq + 1) * MOBA_BLOCK)
        qa = qa_ref[rows, :]
        m_part = None
        for j in range(n_keys):
            s = _dot_t(qa, ka_ref[j * MOBA_BLOCK:(j + 1) * MOBA_BLOCK, :])
            if j == bq:
                s = s + tri_ref[...]
            s_ref[par, j] = s
            mj = jnp.maximum(s[:, :LANES], s[:, LANES:])
            m_part = mj if m_part is None else jnp.maximum(m_part, mj)
        m = jnp.broadcast_to(jnp.max(m_part, axis=-1, keepdims=True), (MOBA_BLOCK, LANES))
        l_part = jnp.zeros((MOBA_BLOCK, LANES), F32)
        for j in range(n_keys):
            for half in range(MOBA_BLOCK // LANES):
                cols = slice(half * LANES, (half + 1) * LANES)
                p = jnp.exp2(s_ref[par, j, :, cols] - m)
                l_part = l_part + p
                p_ref[par, :, j * MOBA_BLOCK + half * LANES:j * MOBA_BLOCK + (half + 1) * LANES] = p.astype(BF16)
        l = jnp.sum(l_part, axis=-1, keepdims=True)
        o = _dot(p_ref[par, :, 0:n_keys * MOBA_BLOCK], vb_ref[0:n_keys * MOBA_BLOCK, :]) / l
        o_ref[rows, :] = o.astype(o_ref.dtype)


def _moba_prompt(q, k, v):
    b, h, s, _ = q.shape
    nb = s // MOBA_BLOCK
    assert nb <= LANES
    nbp = -(-nb // SUBLANES) * SUBLANES
    blk = pl.BlockSpec((1, 1, s, HEAD_DIM), lambda bi, hi: (bi, hi, 0, 0))
    return pl.pallas_call(
        functools.partial(_moba_prompt_kernel, seq=s),
        grid=(b, h),
        in_specs=[blk, blk, blk],
        out_specs=pl.BlockSpec((s, HEAD_DIM), lambda bi, hi: (bi, hi)),
        out_shape=jax.ShapeDtypeStruct((b * s, h * HEAD_DIM), BF16),
        scratch_shapes=[pltpu.VMEM((s, 2 * HEAD_DIM), BF16), pltpu.VMEM((s, 2 * HEAD_DIM), BF16),
                        pltpu.VMEM((s, HEAD_DIM), BF16),
                        pltpu.VMEM((LANES, HEAD_DIM), F32), pltpu.VMEM((nbp, s), F32), pltpu.VMEM((LANES, s), F32),
                        pltpu.VMEM((MOBA_BLOCK, MOBA_BLOCK), F32),
                        pltpu.VMEM((2, nb, MOBA_BLOCK, MOBA_BLOCK), F32), pltpu.VMEM((2, MOBA_BLOCK, s), BF16)],
        compiler_params=_params("arbitrary", "arbitrary"),
        name="moba_prompt",
    )(q, k, v)


def _mem_attend_kernel(q_ref, k_ref, v_ref, o_ref, *, seqs, rows):
    for sq in range(seqs):
        rsl = slice(sq * rows, (sq + 1) * rows)
        for h in range(MEM_HEADS):
            sl = slice(h * HEAD_DIM, (h + 1) * HEAD_DIM)
            head = (lambda ref: ref[sq, :, h, :]) if len(k_ref.shape) == 4 else (lambda ref: ref[sq, :, sl])
            qs = (q_ref[rsl, sl].astype(F32) * SCALE).astype(BF16)
            s = _dot_t(qs, head(k_ref).astype(BF16))
            m = jnp.max(s, axis=-1, keepdims=True)
            p = jnp.exp(s - m)
            l = jnp.sum(p, axis=-1, keepdims=True)
            o = _dot(p.astype(BF16), head(v_ref).astype(BF16)) / l
            o_ref[rsl, sl] = o.astype(o_ref.dtype)


def _mem_attend(qm, mem_k, mem_v, *, rows_per_seq, tq, seqs_per_step=1):
    n = qm.shape[0]
    m_tok = mem_k.shape[1]
    tiles = rows_per_seq // tq
    assert seqs_per_step == 1 or tiles == 1
    rows = tq * seqs_per_step
    kv = pl.BlockSpec((seqs_per_step,) + mem_k.shape[1:], lambda i: (i // tiles,) + (0,) * (mem_k.ndim - 1))
    return pl.pallas_call(
        functools.partial(_mem_attend_kernel, seqs=seqs_per_step, rows=tq),
        grid=(n // rows,),
        in_specs=[pl.BlockSpec((rows, D_MEM), lambda i: (i, 0)), kv, kv],
        out_specs=pl.BlockSpec((rows, D_MEM), lambda i: (i, 0)),
        out_shape=jax.ShapeDtypeStruct((n, D_MEM), BF16),
        compiler_params=_params("arbitrary"),
        name="mem_attend",
    )(qm, mem_k, mem_v)


def _mem_kv_kernel(x_ref, w_ref, g_ref, k_ref, v_ref):
    kv = _dot(x_ref[...].astype(BF16), w_ref[...])
    for h in range(MEM_HEADS):
        sl = slice(h * HEAD_DIM, (h + 1) * HEAD_DIM)
        k_ref[:, sl] = _rms(kv[:, sl], g_ref[...])
    v_ref[...] = kv[:, D_MEM:]


def _mem_kv(mem2d, w_b, g, *, tm):
    n = mem2d.shape[0]
    out = jax.ShapeDtypeStruct((n, D_MEM), F32)
    return pl.pallas_call(
        _mem_kv_kernel,
        grid=(n // tm,),
        in_specs=[pl.BlockSpec((tm, D_MODEL), lambda i: (i, 0)),
                  pl.BlockSpec((D_MODEL, 2 * D_MEM), lambda i: (0, 0)),
                  pl.BlockSpec((1, HEAD_DIM), lambda i: (0, 0))],
        out_specs=[pl.BlockSpec((tm, D_MEM), lambda i: (i, 0))] * 2,
        out_shape=[out, out],
        compiler_params=_params("arbitrary"),
        name="mem_kv",
    )(mem2d, w_b, g)


def _merge_kernel(x_ref, g1_ref, yp_ref, ya_ref, ym_ref, wg0_ref, wg1_ref, wg2_ref,
                  wbp_ref, wba_ref, wbm_ref, wo_ref, o_ref, hb_ref, acc_ref):
    c = pl.program_id(1)

    @pl.when(c == 0)
    def _():
        hb_ref[...] = _rms(x_ref[...], g1_ref[...]).astype(BF16)
        acc_ref[...] = jnp.zeros(acc_ref.shape, F32)

    hb = hb_ref[...]
    merged = (jax.nn.sigmoid(_dot(hb, wg0_ref[...])) * _dot(yp_ref[...].astype(BF16), wbp_ref[...])
              + jax.nn.sigmoid(_dot(hb, wg1_ref[...])) * _dot(ya_ref[...].astype(BF16), wba_ref[...])
              + jax.nn.sigmoid(_dot(hb, wg2_ref[...])) * _dot(ym_ref[...].astype(BF16), wbm_ref[...]))
    acc_ref[...] += _dot(merged.astype(BF16), wo_ref[...])

    @pl.when(c == pl.num_programs(1) - 1)
    def _():
        o_ref[...] = x_ref[...] + acc_ref[...]


def _merge(x2d, g1, yp, ya, ym, w_in_b, wbp_b, wba_b, wbm_b, wo_b, *, tm):
    n = x2d.shape[0]
    tc = MERGE_CHUNK
    n_chunks = D_MODEL // tc
    gate0 = D_QKV // tc
    row = lambda i, c: (i, 0)

    def gate_spec(branch):
        return pl.BlockSpec((D_MODEL, tc), lambda i, c: (0, gate0 + branch * n_chunks + c))

    col = lambda rows: pl.BlockSpec((rows, tc), lambda i, c: (0, c))
    return pl.pallas_call(
        _merge_kernel,
        grid=(n // tm, n_chunks),
        in_specs=[pl.BlockSpec((tm, D_MODEL), row), pl.BlockSpec((1, D_MODEL), lambda i, c: (0, 0)),
                  pl.BlockSpec((tm, D_POOL), row), pl.BlockSpec((tm, D_ATTN), row), pl.BlockSpec((tm, D_MEM), row),
                  gate_spec(0), gate_spec(1), gate_spec(2),
                  col(D_POOL), col(D_ATTN), col(D_MEM),
                  pl.BlockSpec((tc, D_MODEL), lambda i, c: (c, 0))],
        out_specs=pl.BlockSpec((tm, D_MODEL), row),
        out_shape=jax.ShapeDtypeStruct((n, D_MODEL), F32),
        scratch_shapes=[pltpu.VMEM((tm, D_MODEL), BF16), pltpu.VMEM((tm, D_MODEL), F32)],
        compiler_params=_params("arbitrary", "arbitrary"),
        name="merge",
    )(x2d, g1, yp, ya, ym, w_in_b, w_in_b, w_in_b, wbp_b, wba_b, wbm_b, wo_b)


def _ffn_act(ua, ug, a_m1, a_m2, g_m1, g_m2, cwa_ref, cwg_ref, cba_ref, cbg_ref):
    ca = cba_ref[...] + cwa_ref[0:1, :] * a_m2 + cwa_ref[1:2, :] * a_m1 + cwa_ref[2:3, :] * ua
    cg = cbg_ref[...] + cwg_ref[0:1, :] * g_m2 + cwg_ref[1:2, :] * g_m1 + cwg_ref[2:3, :] * ug
    return (jax.nn.silu(ca) * cg).astype(BF16)


def _side_page_copy(pt_ref, cache_ref, buf_ref, sem_ref, group, k, *, n_pages, total_pages, pages_per_step):
    idx = jnp.minimum(group * pages_per_step + k, total_pages - 1)
    page = pt_ref[lax.div(idx, n_pages), lax.rem(idx, n_pages)]
    half = lax.rem(group, 2)
    return pltpu.make_async_copy(cache_ref.at[0, page], buf_ref.at[half, k], sem_ref.at[half, k])


def _ffn_seq_kernel(pt_ref, xu_ref, g2_ref, wua_ref, wug_ref, cwa_ref, cwg_ref, cba_ref, cbg_ref, wd_ref, cache_ref,
                    o_ref, sta_ref, stg_ref, psum_ref,
                    hb_ref, exta_ref, extg_ref, carry_ref, pbuf_ref, psem_ref,
                    *, tm, tiles_per_seq, n_chunks, n_steps, n_pages, total_pages, pages_per_step, n_groups):
    t = pl.program_id(0)
    tu = jnp.minimum(t, n_steps - 1)
    i, c = tu // n_chunks, tu % n_chunks
    n_prev = CONV_WIDTH - 1
    lo = CONV_HALO - n_prev
    page_copy = functools.partial(_side_page_copy, pt_ref, cache_ref, pbuf_ref, psem_ref,
                                  n_pages=n_pages, total_pages=total_pages, pages_per_step=pages_per_step)

    @pl.when(t == 0)
    def _():
        exta_ref[1] = jnp.zeros(exta_ref.shape[1:], F32)
        extg_ref[1] = jnp.zeros(extg_ref.shape[1:], F32)
        pbuf_ref[...] = jnp.zeros(pbuf_ref.shape, F32)
        for k in range(pages_per_step):
            page_copy(0, k).start(priority=PAGE_DMA_PRIORITY)

    @pl.when(t + 1 < n_groups)
    def _():
        for k in range(pages_per_step):
            page_copy(t + 1, k).start(priority=PAGE_DMA_PRIORITY)

    @pl.when(t < n_groups)
    def _():
        for k in range(pages_per_step):
            page_copy(t, k).wait()

    @pl.when(c == 0)
    def _():
        hb_ref[...] = _rms(xu_ref[...], g2_ref[...]).astype(BF16)

    @pl.when((t == 0) | (t % n_chunks == 1))
    def _():
        o_ref[...] = xu_ref[...]

    @pl.when(i % tiles_per_seq == 0)
    def _():
        carry_ref[c] = jnp.zeros(carry_ref.shape[1:], F32)

    cur, prv = t % 2, (t + 1) % 2
    hb = hb_ref[...]
    ua = _dot(hb, wua_ref[...])
    ug = _dot(hb, wug_ref[...])
    exta_ref[cur, lo:CONV_HALO, :] = carry_ref[c, 0:n_prev, :]
    extg_ref[cur, lo:CONV_HALO, :] = carry_ref[c, n_prev:2 * n_prev, :]
    exta_ref[cur, CONV_HALO:CONV_HALO + tm, :] = ua
    extg_ref[cur, CONV_HALO:CONV_HALO + tm, :] = ug
    a_last = ua[tm - n_prev:tm, :]
    g_last = ug[tm - n_prev:tm, :]
    carry_ref[c, 0:n_prev, :] = a_last
    carry_ref[c, n_prev:2 * n_prev, :] = g_last

    rows = lambda ref, back: ref[prv, CONV_HALO - back:CONV_HALO - back + tm, :]
    act = _ffn_act(rows(exta_ref, 0), rows(extg_ref, 0), rows(exta_ref, 1), rows(exta_ref, 2),
                   rows(extg_ref, 1), rows(extg_ref, 2), cwa_ref, cwg_ref, cba_ref, cbg_ref)
    o_ref[...] += _dot(act, wd_ref[...])

    for k in range(pages_per_step):
        psum_ref[k] = jnp.sum(pbuf_ref[cur, k], axis=1)

    @pl.when(i % tiles_per_seq == tiles_per_seq - 1)
    def _():
        sta_ref[i // tiles_per_seq, c] = a_last
        stg_ref[i // tiles_per_seq, c] = g_last


def _ffn_step_kernel(x_ref, g2_ref, wua_ref, wug_ref, cwa_ref, cwg_ref, cba_ref, cbg_ref, wd_ref,
                     am2_ref, gm2_ref, am1_ref, gm1_ref,
                     o_ref, ua_ref, ug_ref, hb_ref, acc_ref):
    c = pl.program_id(1)

    @pl.when(c == 0)
    def _():
        hb_ref[...] = _rms(x_ref[...], g2_ref[...]).astype(BF16)
        acc_ref[...] = jnp.zeros(acc_ref.shape, F32)

    hb = hb_ref[...]
    ua = _dot(hb, wua_ref[...])
    ug = _dot(hb, wug_ref[...])
    ua_ref[...] = ua
    ug_ref[...] = ug
    act = _ffn_act(ua, ug, am1_ref[...], am2_ref[...], gm1_ref[...], gm2_ref[...], cwa_ref, cwg_ref, cba_ref, cbg_ref)
    acc_ref[...] += _dot(act, wd_ref[...])

    @pl.when(c == pl.num_programs(1) - 1)
    def _():
        o_ref[...] = x_ref[...] + acc_ref[...]


def _ffn_seq(x2d, g2, w_up_b, conv_w, conv_b2, w_down_b, page_table, cache_k, *, tm, batch, seq):
    n = x2d.shape[0]
    tc = FFN_CHUNK
    n_chunks = D_FF // tc
    assert n_chunks >= 2
    tiles_per_seq = seq // tm
    n_prev = CONV_WIDTH - 1
    n_steps = (n // tm) * n_chunks
    db, n_pages = page_table.shape
    total_pages = db * n_pages
    pages_per_step = -(-total_pages // n_steps)
    n_groups = -(-total_pages // pages_per_step)
    up_tile = lambda t: jnp.minimum(t, n_steps - 1) // n_chunks
    up_chunk = lambda t: jnp.minimum(t, n_steps - 1) % n_chunks
    down_tile = lambda t: jnp.maximum(t - 1, 0) // n_chunks
    down_chunk = lambda t: jnp.maximum(t - 1, 0) % n_chunks
    a_up = lambda rows: pl.BlockSpec((rows, tc), lambda t, pt: (0, up_chunk(t)))
    g_up = lambda rows: pl.BlockSpec((rows, tc), lambda t, pt: (0, n_chunks + up_chunk(t)))
    a_dn = lambda rows: pl.BlockSpec((rows, tc), lambda t, pt: (0, down_chunk(t)))
    g_dn = lambda rows: pl.BlockSpec((rows, tc), lambda t, pt: (0, n_chunks + down_chunk(t)))
    st_spec = pl.BlockSpec((batch, n_chunks, n_prev, tc), lambda t, pt: (0, 0, 0, 0))
    st_shape = jax.ShapeDtypeStruct((batch, n_chunks, n_prev, tc), F32)
    ps_spec = pl.BlockSpec((pages_per_step, N_HEADS, HEAD_DIM), lambda t, pt: (jnp.minimum(t, n_groups), 0, 0))
    ps_shape = jax.ShapeDtypeStruct(((n_groups + 1) * pages_per_step, N_HEADS, HEAD_DIM), F32)
    y, sta, stg, psums = pl.pallas_call(
        functools.partial(_ffn_seq_kernel, tm=tm, tiles_per_seq=tiles_per_seq, n_chunks=n_chunks, n_steps=n_steps,
                          n_pages=n_pages, total_pages=total_pages, pages_per_step=pages_per_step, n_groups=n_groups),
        grid_spec=pltpu.PrefetchScalarGridSpec(
            num_scalar_prefetch=1,
            grid=(n_steps + 1,),
            in_specs=[pl.BlockSpec((tm, D_MODEL), lambda t, pt: (up_tile(t), 0)),
                      pl.BlockSpec((1, D_MODEL), lambda t, pt: (0, 0)),
                      a_up(D_MODEL), g_up(D_MODEL), a_dn(CONV_WIDTH), g_dn(CONV_WIDTH), a_dn(1), g_dn(1),
                      pl.BlockSpec((tc, D_MODEL), lambda t, pt: (down_chunk(t), 0)),
                      pl.BlockSpec(memory_space=pl.ANY)],
            out_specs=[pl.BlockSpec((tm, D_MODEL), lambda t, pt: (down_tile(t), 0)), st_spec, st_spec, ps_spec],
            scratch_shapes=[pltpu.VMEM((tm, D_MODEL), BF16),
                            pltpu.VMEM((2, tm + CONV_HALO, tc), F32), pltpu.VMEM((2, tm + CONV_HALO, tc), F32),
                            pltpu.VMEM((n_chunks, SUBLANES, tc), F32),
                            pltpu.VMEM((2, pages_per_step, N_HEADS, PAGE_SIZE, HEAD_DIM), F32),
                            pltpu.SemaphoreType.DMA((2, pages_per_step))],
        ),
        out_shape=[jax.ShapeDtypeStruct((n, D_MODEL), F32), st_shape, st_shape, ps_shape],
        compiler_params=_params("arbitrary"),
        name="ffn_seq",
    )(page_table, x2d, g2, w_up_b, w_up_b, conv_w, conv_w, conv_b2, conv_b2, w_down_b, cache_k)
    unchunk = lambda st: st.transpose(0, 2, 1, 3).reshape(batch, n_prev, D_FF)
    page_sums = psums[:total_pages].reshape(db, n_pages, N_HEADS, HEAD_DIM).transpose(0, 2, 1, 3)
    return y, unchunk(sta), unchunk(stg), page_sums


def _ffn_step(x2d, g2, w_up_b, conv_w, conv_b2, w_down_b, prev2, prev1, *, tm):
    n = x2d.shape[0]
    tc = FFN_CHUNK
    n_chunks = D_FF // tc
    row = lambda i, c: (i, 0)
    a_col = lambda rows: pl.BlockSpec((rows, tc), lambda i, c: (0, c))
    g_col = lambda rows: pl.BlockSpec((rows, tc), lambda i, c: (0, n_chunks + c))
    rows_a = pl.BlockSpec((tm, tc), lambda i, c: (i, c))
    rows_g = pl.BlockSpec((tm, tc), lambda i, c: (i, n_chunks + c))
    up_shape = jax.ShapeDtypeStruct((n, D_FF), F32)
    return pl.pallas_call(
        _ffn_step_kernel,
        grid=(n // tm, n_chunks),
        in_specs=[pl.BlockSpec((tm, D_MODEL), row), pl.BlockSpec((1, D_MODEL), lambda i, c: (0, 0)),
                  a_col(D_MODEL), g_col(D_MODEL), a_col(CONV_WIDTH), g_col(CONV_WIDTH), a_col(1), g_col(1),
                  pl.BlockSpec((tc, D_MODEL), lambda i, c: (c, 0)),
                  rows_a, rows_g, rows_a, rows_g],
        out_specs=[pl.BlockSpec((tm, D_MODEL), row), rows_a, rows_a],
        out_shape=[jax.ShapeDtypeStruct((n, D_MODEL), F32), up_shape, up_shape],
        scratch_shapes=[pltpu.VMEM((tm, D_MODEL), BF16), pltpu.VMEM((tm, D_MODEL), F32)],
        compiler_params=_params("arbitrary", "arbitrary"),
        name="ffn_step",
    )(x2d, g2, w_up_b, w_up_b, conv_w, conv_w, conv_b2, conv_b2, w_down_b, prev2, prev2, prev1, prev1)


def _block_gate_kernel(q_ref, ps_ref, o_ref, km_ref, *, n_blocks, seqs):
    km_ref[...] = jnp.zeros(km_ref.shape, F32)
    lane = lax.broadcasted_iota(jnp.int32, (N_HEADS, LANES), 1)
    row = lax.broadcasted_iota(jnp.int32, (N_HEADS, LANES), 0)

    def one_seq(sq, carry):
        gs_all = jnp.full((N_HEADS, LANES), NEG, F32)
        for h in range(N_HEADS):
            even = ps_ref[sq, h, pl.ds(0, n_blocks, stride=PAGES_PER_BLOCK), :]
            odd = ps_ref[sq, h, pl.ds(1, n_blocks, stride=PAGES_PER_BLOCK), :]
            km_ref[h, 0:n_blocks, :] = (even + odd) * (1.0 / MOBA_BLOCK)
            qh = jnp.broadcast_to(q_ref[sq, :, h * HEAD_DIM:(h + 1) * HEAD_DIM], (N_HEADS, HEAD_DIM))
            gs = _dot_t(qh.astype(BF16), km_ref[h].astype(BF16))
            gs_all = jnp.where(row == h, gs, gs_all)
        gs_all = jnp.where(lane < n_blocks, gs_all, NEG)
        i0, i1, i2 = _top3(gs_all, n_blocks)
        o_ref[sq] = jnp.where(lane == 0, i0, jnp.where(lane == 1, i1, jnp.where(lane == 2, i2, 0)))
        return carry

    lax.fori_loop(0, seqs, one_seq, 0)


def _block_gate(q3, page_sums):
    db, _, n_pages, _ = page_sums.shape
    n_blocks = n_pages // PAGES_PER_BLOCK
    assert PAGES_PER_BLOCK == 2 and MOBA_TOPK <= n_blocks <= LANES and N_HEADS == SUBLANES
    seqs = _row_tile(db, SUBLANES)
    return pl.pallas_call(
        functools.partial(_block_gate_kernel, n_blocks=n_blocks, seqs=seqs),
        grid=(db // seqs,),
        in_specs=[pl.BlockSpec((seqs, 1, D_ATTN), lambda b: (b, 0, 0)),
                  pl.BlockSpec((seqs, N_HEADS, n_pages, HEAD_DIM), lambda b: (b, 0, 0, 0))],
        out_specs=pl.BlockSpec((seqs, N_HEADS, LANES), lambda b: (b, 0, 0)),
        out_shape=jax.ShapeDtypeStruct((db, N_HEADS, LANES), jnp.int32),
        scratch_shapes=[pltpu.VMEM((N_HEADS, LANES, HEAD_DIM), F32)],
        compiler_params=_params("arbitrary"),
        name="block_gate",
    )(q3, page_sums)


N_SEL_PAGES = MOBA_TOPK * PAGES_PER_BLOCK


def _sel_copy(pt_ref, sel_ref, cache_ref, buf_ref, sem_ref, b, h, s):
    blk = sel_ref[b, h * MOBA_TOPK + s // PAGES_PER_BLOCK]
    page = pt_ref[b, blk * PAGES_PER_BLOCK + s % PAGES_PER_BLOCK]
    half = lax.rem(b, 2)
    return pltpu.make_async_copy(cache_ref.at[0, page, h], buf_ref.at[half, h, s], sem_ref.at[half, h])


def _moba_sample_kernel(pt_ref, sel_ref, q_ref, kn_ref, vn_ref, ck_ref, cv_ref, o_ref,
                        kbuf_ref, vbuf_ref, ksem_ref, vsem_ref):
    b = pl.program_id(0)

    def fetch(seq):
        for h in range(N_HEADS):
            for s in range(N_SEL_PAGES):
                _sel_copy(pt_ref, sel_ref, ck_ref, kbuf_ref, ksem_ref, seq, h, s).start()
                _sel_copy(pt_ref, sel_ref, cv_ref, vbuf_ref, vsem_ref, seq, h, s).start()

    @pl.when(b == 0)
    def _():
        fetch(b)

    @pl.when(b + 1 < pl.num_programs(0))
    def _():
        fetch(b + 1)

    half = lax.rem(b, 2)
    for h in range(N_HEADS):
        for s in range(N_SEL_PAGES):
            _sel_copy(pt_ref, sel_ref, ck_ref, kbuf_ref, ksem_ref, b, h, s).wait()
            _sel_copy(pt_ref, sel_ref, cv_ref, vbuf_ref, vsem_ref, b, h, s).wait()
        sl = slice(h * HEAD_DIM, (h + 1) * HEAD_DIM)
        qs = q_ref[0, :, sl] * SCALE
        kk = kbuf_ref[half, h].reshape(N_SEL_PAGES * PAGE_SIZE, HEAD_DIM)
        vv = vbuf_ref[half, h].reshape(N_SEL_PAGES * PAGE_SIZE, HEAD_DIM)
        s_sel = jnp.sum(kk * qs, axis=-1, keepdims=True)
        s_own = jnp.sum(qs * kn_ref[0, :, sl], axis=-1, keepdims=True)
        m = jnp.maximum(jnp.max(s_sel, axis=0, keepdims=True), s_own)
        p = jnp.exp(s_sel - m)
        p_own = jnp.exp(s_own - m)
        l = jnp.sum(p, axis=0, keepdims=True) + p_own
        o = (jnp.sum(p * vv, axis=0, keepdims=True) + p_own * vn_ref[0, :, sl]) / l
        o_ref[0, :, sl] = o.astype(o_ref.dtype)


def _moba_sample(page_table, sel, q3, k3, v3, cache_k, cache_v):
    db = q3.shape[0]
    tok = pl.BlockSpec((1, 1, D_ATTN), lambda b, pt, sl: (b, 0, 0))
    return pl.pallas_call(
        _moba_sample_kernel,
        grid_spec=pltpu.PrefetchScalarGridSpec(
            num_scalar_prefetch=2,
            grid=(db,),
            in_specs=[tok, tok, tok, pl.BlockSpec(memory_space=pl.ANY), pl.BlockSpec(memory_space=pl.ANY)],
            out_specs=tok,
            scratch_shapes=[pltpu.VMEM((2, N_HEADS, N_SEL_PAGES, PAGE_SIZE, HEAD_DIM), F32),
                            pltpu.VMEM((2, N_HEADS, N_SEL_PAGES, PAGE_SIZE, HEAD_DIM), F32),
                            pltpu.SemaphoreType.DMA((2, N_HEADS)), pltpu.SemaphoreType.DMA((2, N_HEADS))],
        ),
        out_shape=jax.ShapeDtypeStruct((db, 1, D_ATTN), F32),
        compiler_params=_params("arbitrary"),
        name="moba_sample",
    )(page_table, sel, q3, k3, v3, cache_k, cache_v)


def _row_tile(n, pref):
    return pref if n % pref == 0 else n


def kernel(x_prompt, x_sample, mem_prompt, cache_k, cache_v, page_table, cache_mem_k, cache_mem_v, state_pool, state_conv, norm1_g, w_in, q_norm_g, k_norm_g, mem_q_norm_g, mem_k_norm_g, w_mem_kv, pool_w, pool_scale, w_branch_pool, w_branch_attn, w_branch_mem, w_out, norm2_g, w_up, conv_w, conv_b, w_down):
    b, s, _ = x_prompt.shape
    db, t, _ = x_sample.shape
    assert t == 1 and norm1_g.shape[0] == 1 and s % MOBA_BLOCK == 0
    n_pages = page_table.shape[1]
    assert n_pages % PAGES_PER_BLOCK == 0
    m_tok = mem_prompt.shape[1]

    w_in_b = w_in[0].astype(BF16)
    wbp_b, wba_b, wbm_b = (w[0].astype(BF16) for w in (w_branch_pool, w_branch_attn, w_branch_mem))
    wo_b, w_up_b, w_down_b = w_out[0].astype(BF16), w_up[0].astype(BF16), w_down[0].astype(BF16)
    w_mem_b, pool_w_b = w_mem_kv[0].astype(BF16), pool_w[0].astype(BF16)
    g1, g2 = norm1_g, norm2_g
    conv_b2 = conv_b
    norm_args = (q_norm_g, k_norm_g, mem_q_norm_g, pool_w_b, pool_scale)

    n = b * s
    x2d = x_prompt.reshape(n, D_MODEL)
    tm = _row_tile(s, 512)
    tables_p = _rope_tables(jnp.arange(s, dtype=jnp.int32))
    u_p, q_p, k_p, v_p, qm_p, yp_p = _inproj_prompt(x2d, g1, w_in_b, tables_p, *norm_args, batch=b, seq=s, tm=tm)
    mk2d, mv2d = _mem_kv(mem_prompt.reshape(b * m_tok, D_MODEL), w_mem_b, mem_k_norm_g, tm=m_tok)
    ya_p = _moba_prompt(q_p, k_p, v_p)
    mk_p = mk2d.reshape(b, m_tok, MEM_HEADS, HEAD_DIM)
    mv_p = mv2d.reshape(b, m_tok, MEM_HEADS, HEAD_DIM)
    ym_p = _mem_attend(qm_p, mk2d.reshape(b, m_tok, D_MEM), mv2d.reshape(b, m_tok, D_MEM),
                       rows_per_seq=s, tq=_row_tile(s, 1024))
    x1_p = _merge(x2d, g1, yp_p, ya_p, ym_p, w_in_b, wbp_b, wba_b, wbm_b, wo_b, tm=tm)
    y_p, sta_p, stg_p, sums = _ffn_seq(x1_p, g2, w_up_b, conv_w[0], conv_b2, w_down_b, page_table, cache_k,
                                       tm=tm, batch=b, seq=s)

    xs2d = x_sample.reshape(db, D_MODEL)
    past_len = n_pages * PAGE_SIZE
    pos_s = jnp.full((db,), past_len, jnp.int32)
    u_s, q_s, k_s, v_s, qm_s, yp_s = _inproj_sample(xs2d, g1, w_in_b, _rope_tables(pos_s), *norm_args, state_pool[0],
                                                    past_len=past_len)
    q3, k3, v3 = (a.reshape(db, 1, D_ATTN) for a in (q_s, k_s, v_s))
    sel = _block_gate(q3, sums)[:, :, :MOBA_TOPK].reshape(db, N_HEADS * MOBA_TOPK)
    ya_s = _moba_sample(page_table, sel, q3, k3, v3, cache_k, cache_v).reshape(db, D_ATTN)
    qm_rep = jnp.broadcast_to(qm_s[:, None, :], (db, BF16_ROWS, D_MEM)).reshape(db * BF16_ROWS, D_MEM)
    ym_s = _mem_attend(qm_rep, cache_mem_k[0], cache_mem_v[0], rows_per_seq=BF16_ROWS, tq=BF16_ROWS,
                       seqs_per_step=_row_tile(db, SUBLANES)).reshape(db, BF16_ROWS, D_MEM)[:, 0]
    x1_s = _merge(xs2d, g1, yp_s, ya_s, ym_s, w_in_b, wbp_b, wba_b, wbm_b, wo_b, tm=db)
    y_s, ua_s, ug_s = _ffn_step(x1_s, g2, w_up_b, conv_w[0], conv_b2, w_down_b,
                                state_conv[0, :, 0], state_conv[0, :, 1], tm=db)

    up_s = jnp.concatenate([ua_s, ug_s], axis=-1)
    return (
        y_p.reshape(b, s, D_MODEL),
        y_s.reshape(db, 1, D_MODEL),
        k_p[None], v_p[None],
        k_s.reshape(1, db, N_HEADS, 1, HEAD_DIM), v_s.reshape(1, db, N_HEADS, 1, HEAD_DIM),
        mk_p[None], mv_p[None],
        u_p.reshape(b, s, D_POOL)[None, :, s - POOL_STATE:],
        jnp.concatenate([state_pool[0][:, 1:], u_s[:, None]], axis=1)[None],
        jnp.concatenate([sta_p, stg_p], axis=-1)[None],
        jnp.concatenate([state_conv[0][:, 1:], up_s[:, None]], axis=1)[None],
    )
```

```python
import functools

import jax
import jax.numpy as jnp
import numpy as np
from jax import lax
from jax.experimental import pallas as pl
from jax.experimental.pallas import tpu as pltpu

F32 = jnp.float32
BF16 = jnp.bfloat16

D_MODEL = 2048
HEAD_DIM = 128
N_HEADS = 8
MEM_HEADS = 4
POOL_WINDOWS = (2, 4, 8, 16)
POOL_GROUP_DIM = 128
D_POOL = len(POOL_WINDOWS) * POOL_GROUP_DIM
POOL_STATE = max(POOL_WINDOWS) - 1
D_ATTN = N_HEADS * HEAD_DIM
D_MEM = MEM_HEADS * HEAD_DIM
D_QKV = D_POOL + 3 * D_ATTN + D_MEM
N_BRANCH = 3
MOBA_BLOCK = 256
MOBA_TOPK = 3
PAGE_SIZE = 128
PAGES_PER_BLOCK = MOBA_BLOCK // PAGE_SIZE
ROPE_THETA = 500000.0
ROPE_DIM = HEAD_DIM // 4
ROPE_HALF = ROPE_DIM // 2
D_FF = 5632
CONV_WIDTH = 3
EPS = 1e-6
NEG = -1e30
SCALE = HEAD_DIM ** -0.5
LOG2E = 1.4426950408889634

LANES = 128
SUBLANES = 8
BF16_ROWS = 2 * SUBLANES
POOL_HALO = 16
CONV_HALO = SUBLANES
VMEM_LIMIT = 56 * 1024 * 1024
FFN_CHUNK = 512
MERGE_CHUNK = 512
PAGE_DMA_PRIORITY = 1


def _params(*sem):
    return pltpu.CompilerParams(dimension_semantics=sem, vmem_limit_bytes=VMEM_LIMIT)


def _rms(x, g):
    ms = jnp.mean(x * x, axis=-1, keepdims=True)
    return x * lax.rsqrt(ms + EPS) * g


def _rope(t, cos, sin_lo, sin_hi):
    return (t * cos + pltpu.roll(t, ROPE_HALF, 1) * sin_hi
            + pltpu.roll(t, HEAD_DIM - ROPE_HALF, 1) * sin_lo)


def _rope_tables(pos):
    half = ROPE_HALF
    inv = ROPE_THETA ** (-jnp.arange(half, dtype=F32) * (2.0 / ROPE_DIM))
    ang = pos.astype(F32)[:, None] * inv[None, :]
    cos, sin = jnp.cos(ang), jnp.sin(ang)
    n = pos.shape[0]
    pad = jnp.zeros((n, HEAD_DIM - ROPE_DIM), F32)
    zero = jnp.zeros((n, half), F32)
    cos_t = jnp.concatenate([cos, cos, pad + 1.0], axis=1)
    sin_lo = jnp.concatenate([-sin, zero, pad], axis=1)
    sin_hi = jnp.concatenate([zero, sin, pad], axis=1)
    return cos_t, sin_lo, sin_hi


def _dot(a, b):
    return jnp.dot(a, b, preferred_element_type=F32)


def _dot_t(a, b):
    return lax.dot_general(a, b, (((1,), (1,)), ((), ())), preferred_element_type=F32)


def _top3(gs, n_valid):
    lane = lax.broadcasted_iota(jnp.int32, gs.shape, 1)
    out = []
    for r in range(MOBA_TOPK):
        mx = jnp.max(gs, axis=-1, keepdims=True)
        idx = jnp.min(jnp.where(gs == mx, lane, LANES), axis=-1, keepdims=True)
        out.append(jnp.where(r < n_valid, idx, -1))
        gs = jnp.where(lane == idx, -jnp.inf, gs)
    return out


def _pool_branch(u, win_sum, cnt, pw_ref, pscale_ref, yp_ref):
    for g, w in enumerate(POOL_WINDOWS):
        sl = slice(g * POOL_GROUP_DIM, (g + 1) * POOL_GROUP_DIM)
        p = win_sum(g, w) / cnt(w) - u[:, sl]
        y = _dot(p.astype(BF16), pw_ref[g]) * pscale_ref[:, sl]
        yp_ref[:, sl] = y.astype(yp_ref.dtype)


def _inproj_prompt_kernel(x_ref, g1_ref, w_ref, cos_ref, slo_ref, shi_ref, qg_ref, kg_ref, mqg_ref,
                          pw_ref, pscale_ref,
                          u_ref, q_ref, k_ref, v_ref, qm_ref, yp_ref, hb_ref,
                          uext_ref, *, tm, tiles_per_seq):
    i = pl.program_id(0)
    t_in_seq = i % tiles_per_seq
    hb = _rms(x_ref[...], g1_ref[...]).astype(BF16)
    hb_ref[...] = hb

    u = _dot(hb, w_ref[:, 0:D_POOL])
    u_ref[...] = u

    @pl.when(t_in_seq == 0)
    def _():
        uext_ref[0:POOL_HALO, :] = jnp.zeros((POOL_HALO, D_POOL), F32)

    uext_ref[POOL_HALO:POOL_HALO + tm, :] = u
    pos = t_in_seq * tm + lax.broadcasted_iota(jnp.int32, (tm, 1), 0)

    def win_sum(g, w):
        sl = slice(g * POOL_GROUP_DIM, (g + 1) * POOL_GROUP_DIM)
        s = uext_ref[POOL_HALO:POOL_HALO + tm, sl]
        for d in range(1, w):
            s = s + uext_ref[POOL_HALO - d:POOL_HALO - d + tm, sl]
        return s

    cos, slo, shi = cos_ref[...], slo_ref[...], shi_ref[...]
    c0 = D_POOL
    qf = _dot(hb, w_ref[:, c0:c0 + D_ATTN])
    for h in range(N_HEADS):
        t = _rms(qf[:, h * HEAD_DIM:(h + 1) * HEAD_DIM], qg_ref[...])
        q_ref[0, h] = _rope(t, cos, slo, shi)
    c0 += D_ATTN
    kf = _dot(hb, w_ref[:, c0:c0 + D_ATTN])
    for h in range(N_HEADS):
        t = _rms(kf[:, h * HEAD_DIM:(h + 1) * HEAD_DIM], kg_ref[...])
        k_ref[0, h] = _rope(t, cos, slo, shi)
    c0 += 2 * D_ATTN
    mf = _dot(hb, w_ref[:, c0:c0 + D_MEM])
    for h in range(MEM_HEADS):
        sl = slice(h * HEAD_DIM, (h + 1) * HEAD_DIM)
        qm_ref[:, sl] = _rms(mf[:, sl], mqg_ref[...]).astype(qm_ref.dtype)
    c0 -= D_ATTN
    vf = _dot(hb, w_ref[:, c0:c0 + D_ATTN])
    for h in range(N_HEADS):
        v_ref[0, h] = vf[:, h * HEAD_DIM:(h + 1) * HEAD_DIM]
    _pool_branch(u, win_sum, lambda w: jnp.minimum(w, pos + 1).astype(F32), pw_ref, pscale_ref, yp_ref)
    uext_ref[0:POOL_HALO, :] = uext_ref[tm:tm + POOL_HALO, :]


def _inproj_prompt(x2d, g1, w_in_b, tables, qg, kg, mqg, pool_w_b, pool_scale, *, batch, seq, tm):
    n = x2d.shape[0]
    tiles_per_seq = seq // tm
    row = lambda i: (i, 0)
    const = lambda i: (0, 0)
    tab = pl.BlockSpec((tm, HEAD_DIM), lambda i: (i % tiles_per_seq, 0))
    head_major = pl.BlockSpec((1, N_HEADS, tm, HEAD_DIM), lambda i: (i // tiles_per_seq, 0, i % tiles_per_seq, 0))
    hm_shape = jax.ShapeDtypeStruct((batch, N_HEADS, seq, HEAD_DIM), F32)
    return pl.pallas_call(
        functools.partial(_inproj_prompt_kernel, tm=tm, tiles_per_seq=tiles_per_seq),
        grid=(n // tm,),
        in_specs=[
            pl.BlockSpec((tm, D_MODEL), row),
            pl.BlockSpec((1, D_MODEL), const),
            pl.BlockSpec((D_MODEL, D_QKV), const, pipeline_mode=pl.Buffered(1)),
            tab, tab, tab,
            pl.BlockSpec((1, HEAD_DIM), const), pl.BlockSpec((1, HEAD_DIM), const), pl.BlockSpec((1, HEAD_DIM), const),
            pl.BlockSpec((len(POOL_WINDOWS), POOL_GROUP_DIM, POOL_GROUP_DIM), lambda i: (0, 0, 0)),
            pl.BlockSpec((1, D_POOL), const),
        ],
        out_specs=[
            pl.BlockSpec((tm, D_POOL), row), head_major, head_major, head_major,
            pl.BlockSpec((tm, D_MEM), row), pl.BlockSpec((tm, D_POOL), row), pl.BlockSpec((tm, D_MODEL), row),
        ],
        out_shape=[
            jax.ShapeDtypeStruct((n, D_POOL), F32), hm_shape, hm_shape, hm_shape,
            jax.ShapeDtypeStruct((n, D_MEM), BF16), jax.ShapeDtypeStruct((n, D_POOL), BF16),
            jax.ShapeDtypeStruct((n, D_MODEL), BF16),
        ],
        scratch_shapes=[pltpu.VMEM((tm + POOL_HALO, D_POOL), F32)],
        compiler_params=_params("arbitrary"),
        name="inproj_prompt",
    )(x2d, g1, w_in_b, *tables, qg, kg, mqg, pool_w_b, pool_scale)


def _inproj_sample_kernel(x_ref, g1_ref, w_ref, cos_ref, slo_ref, shi_ref, qg_ref, kg_ref, mqg_ref,
                          pw_ref, pscale_ref, state_ref,
                          u_ref, q_ref, k_ref, v_ref, qm_ref, yp_ref, hb_ref, *, past_len):
    hb = _rms(x_ref[...], g1_ref[...]).astype(BF16)
    hb_ref[...] = hb
    u = _dot(hb, w_ref[:, 0:D_POOL])
    u_ref[...] = u
    db = state_ref.shape[0]
    srow = lax.broadcasted_iota(jnp.int32, (db, POOL_STATE, POOL_GROUP_DIM), 1)

    def win_sum(g, w):
        sl = slice(g * POOL_GROUP_DIM, (g + 1) * POOL_GROUP_DIM)
        hist = jnp.where(srow >= POOL_STATE - (w - 1), state_ref[:, :, sl], 0.0)
        return u[:, sl] + jnp.sum(hist, axis=1)

    _pool_branch(u, win_sum, lambda w: float(min(w, past_len + 1)), pw_ref, pscale_ref, yp_ref)

    cos, slo, shi = cos_ref[...], slo_ref[...], shi_ref[...]
    c0 = D_POOL
    qf = _dot(hb, w_ref[:, c0:c0 + D_ATTN])
    c0 += D_ATTN
    kf = _dot(hb, w_ref[:, c0:c0 + D_ATTN])
    for h in range(N_HEADS):
        sl = slice(h * HEAD_DIM, (h + 1) * HEAD_DIM)
        q_ref[:, sl] = _rope(_rms(qf[:, sl], qg_ref[...]), cos, slo, shi)
        k_ref[:, sl] = _rope(_rms(kf[:, sl], kg_ref[...]), cos, slo, shi)
    c0 += D_ATTN
    v_ref[...] = _dot(hb, w_ref[:, c0:c0 + D_ATTN])
    c0 += D_ATTN
    mf = _dot(hb, w_ref[:, c0:c0 + D_MEM])
    for h in range(MEM_HEADS):
        sl = slice(h * HEAD_DIM, (h + 1) * HEAD_DIM)
        qm_ref[:, sl] = _rms(mf[:, sl], mqg_ref[...]).astype(qm_ref.dtype)


def _inproj_sample(x2d, g1, w_in_b, tables, qg, kg, mqg, pool_w_b, pool_scale, state_pool, *, past_len):
    db = x2d.shape[0]
    full = lambda shape: pl.BlockSpec(shape, lambda i: (0,) * len(shape))
    return pl.pallas_call(
        functools.partial(_inproj_sample_kernel, past_len=past_len),
        grid=(1,),
        in_specs=[
            full((db, D_MODEL)), full((1, D_MODEL)),
            pl.BlockSpec((D_MODEL, D_QKV), lambda i: (0, 0), pipeline_mode=pl.Buffered(1)),
            full((db, HEAD_DIM)), full((db, HEAD_DIM)), full((db, HEAD_DIM)),
            full((1, HEAD_DIM)), full((1, HEAD_DIM)), full((1, HEAD_DIM)),
            full((len(POOL_WINDOWS), POOL_GROUP_DIM, POOL_GROUP_DIM)), full((1, D_POOL)),
            full((db, POOL_STATE, D_POOL)),
        ],
        out_specs=[full((db, D_POOL)), full((db, D_ATTN)), full((db, D_ATTN)), full((db, D_ATTN)),
                   full((db, D_MEM)), full((db, D_POOL)), full((db, D_MODEL))],
        out_shape=[jax.ShapeDtypeStruct((db, D_POOL), F32), jax.ShapeDtypeStruct((db, D_ATTN), F32),
                   jax.ShapeDtypeStruct((db, D_ATTN), F32), jax.ShapeDtypeStruct((db, D_ATTN), F32),
                   jax.ShapeDtypeStruct((db, D_MEM), F32), jax.ShapeDtypeStruct((db, D_POOL), BF16),
                   jax.ShapeDtypeStruct((db, D_MODEL), BF16)],
        compiler_params=_params("arbitrary"),
        name="inproj_sample",
    )(x2d, g1, w_in_b, *tables, qg, kg, mqg, pool_w_b, pool_scale, state_pool)


def _moba_prompt_kernel(q_ref, k_ref, v_ref, o_ref,
                        qa_ref, ka_ref, vb_ref, km_ref, gs_ref, bt_ref, tri_ref, s_ref, p_ref, *, seq):
    nb = seq // MOBA_BLOCK
    nbp = -(-nb // SUBLANES) * SUBLANES
    shift = MOBA_BLOCK.bit_length() - 1
    q = q_ref[0, 0]
    k = k_ref[0, 0]
    row_blk = lax.shift_right_logical(lax.broadcasted_iota(jnp.int32, (seq, LANES), 0), shift)
    lane = lax.broadcasted_iota(jnp.int32, (seq, LANES), 1)
    ka_ref[:, 0:HEAD_DIM] = k.astype(BF16)
    ka_ref[:, HEAD_DIM:] = jnp.where(lane == row_blk, 1.0, 0.0).astype(BF16)
    vb_ref[:, 0:HEAD_DIM] = v_ref[0, 0].astype(BF16)
    vb_ref[:, HEAD_DIM:] = jnp.ones((seq, HEAD_DIM), BF16)
    qa_ref[:, 0:HEAD_DIM] = (q * (SCALE * LOG2E)).astype(BF16)

    km_ref[...] = jnp.zeros(km_ref.shape, F32)
    km_ref[0:nb, :] = jnp.sum(k.reshape(nb, MOBA_BLOCK, HEAD_DIM), axis=1) * (1.0 / MOBA_BLOCK)
    gst = _dot_t(km_ref[...].astype(BF16), q.astype(BF16))
    jidx = lax.broadcasted_iota(jnp.int32, (nbp, seq), 0)
    qblk = lax.shift_right_logical(lax.broadcasted_iota(jnp.int32, (nbp, seq), 1), shift)
    past = jidx < qblk
    gs = jnp.where(past, gst[0:nbp, :], NEG)
    gs_ref[...] = gs
    rank = jnp.zeros((nbp, seq), F32)
    for jp in range(nb):
        other = gs_ref[jp:jp + 1, :]
        rank = rank + jnp.where(jidx > jp, jnp.where(other >= gs, 1.0, 0.0), jnp.where(other > gs, 1.0, 0.0))
    bt_ref[...] = jnp.zeros(bt_ref.shape, F32)
    bt_ref[0:nbp, :] = jnp.where(past, jnp.where(rank < MOBA_TOPK, 0.0, NEG), jnp.where(jidx == qblk, 0.0, NEG))
    for c in range(seq // MOBA_BLOCK):
        rows = slice(c * MOBA_BLOCK, (c + 1) * MOBA_BLOCK)
        qa_ref[rows, HEAD_DIM:] = bt_ref[:, rows].T.astype(BF16)

    tri_ref[...] = jnp.where(lax.broadcasted_iota(jnp.int32, tri_ref.shape, 1)
                             <= lax.broadcasted_iota(jnp.int32, tri_ref.shape, 0), 0.0, NEG)

    for bq in range(nb):
        par = bq % 2
        n_keys = bq + 1
        rows = slice(bq * MOBA_BLOCK, (bq + 1) * MOBA_BLOCK)
        qa = qa_ref[rows, :]
        m_part = None
        for j in range(n_keys):
            s = _dot_t(qa, ka_ref[j * MOBA_BLOCK:(j + 1) * MOBA_BLOCK, :])
            if j == bq:
                s = s + tri_ref[...]
            s_ref[par, j] = s
            mj = jnp.maximum(s[:, :LANES], s[:, LANES:])
            m_part = mj if m_part is None else jnp.maximum(m_part, mj)
        m = jnp.broadcast_to(jnp.max(m_part, axis=-1, keepdims=True), (MOBA_BLOCK, LANES))
        for j in range(n_keys):
            for half in range(MOBA_BLOCK // LANES):
                cols = slice(half * LANES, (half + 1) * LANES)
                p = jnp.exp2(s_ref[par, j, :, cols] - m)
                p_ref[par, :, j * MOBA_BLOCK + half * LANES:j * MOBA_BLOCK + (half + 1) * LANES] = p.astype(BF16)
        ol = _dot(p_ref[par, :, 0:n_keys * MOBA_BLOCK], vb_ref[0:n_keys * MOBA_BLOCK, :])
        o_ref[rows, :] = (ol[:, 0:HEAD_DIM] / ol[:, HEAD_DIM:]).astype(o_ref.dtype)


def _moba_prompt(q, k, v):
    b, h, s, _ = q.shape
    nb = s // MOBA_BLOCK
    assert nb <= LANES
    nbp = -(-nb // SUBLANES) * SUBLANES
    blk = pl.BlockSpec((1, 1, s, HEAD_DIM), lambda bi, hi: (bi, hi, 0, 0))
    return pl.pallas_call(
        functools.partial(_moba_prompt_kernel, seq=s),
        grid=(b, h),
        in_specs=[blk, blk, blk],
        out_specs=pl.BlockSpec((s, HEAD_DIM), lambda bi, hi: (bi, hi)),
        out_shape=jax.ShapeDtypeStruct((b * s, h * HEAD_DIM), BF16),
        scratch_shapes=[pltpu.VMEM((s, 2 * HEAD_DIM), BF16), pltpu.VMEM((s, 2 * HEAD_DIM), BF16),
                        pltpu.VMEM((s, 2 * HEAD_DIM), BF16),
                        pltpu.VMEM((LANES, HEAD_DIM), F32), pltpu.VMEM((nbp, s), F32), pltpu.VMEM((LANES, s), F32),
                        pltpu.VMEM((MOBA_BLOCK, MOBA_BLOCK), F32),
                        pltpu.VMEM((2, nb, MOBA_BLOCK, MOBA_BLOCK), F32), pltpu.VMEM((2, MOBA_BLOCK, s), BF16)],
        compiler_params=_params("arbitrary", "arbitrary"),
        name="moba_prompt",
    )(q, k, v)


def _mem_attend_kernel(q_ref, k_ref, v_ref, o_ref):
    for h in range(MEM_HEADS):
        sl = slice(h * HEAD_DIM, (h + 1) * HEAD_DIM)
        qs = (q_ref[:, sl].astype(F32) * SCALE).astype(BF16)
        s = _dot_t(qs, k_ref[0, :, sl].astype(BF16))
        m = jnp.max(s, axis=-1, keepdims=True)
        p = jnp.exp(s - m)
        l = jnp.sum(p, axis=-1, keepdims=True)
        o = _dot(p.astype(BF16), v_ref[0, :, sl].astype(BF16)) / l
        o_ref[:, sl] = o.astype(o_ref.dtype)


def _mem_attend(qm, mem_k, mem_v, *, rows_per_seq, tq):
    n = qm.shape[0]
    m_tok = mem_k.shape[1]
    tiles = rows_per_seq // tq
    kv = pl.BlockSpec((1, m_tok, D_MEM), lambda i: (i // tiles, 0, 0))
    return pl.pallas_call(
        _mem_attend_kernel,
        grid=(n // tq,),
        in_specs=[pl.BlockSpec((tq, D_MEM), lambda i: (i, 0)), kv, kv],
        out_specs=pl.BlockSpec((tq, D_MEM), lambda i: (i, 0)),
        out_shape=jax.ShapeDtypeStruct((n, D_MEM), BF16),
        compiler_params=_params("arbitrary"),
        name="mem_attend",
    )(qm, mem_k, mem_v)


def _mem_attend_token_kernel(q_ref, k_ref, v_ref, o_ref, *, seqs):
    for sq in range(seqs):
        for h in range(MEM_HEADS):
            sl = slice(h * HEAD_DIM, (h + 1) * HEAD_DIM)
            qh = q_ref[sq:sq + 1, sl] * SCALE
            s = jnp.sum(k_ref[sq, :, h, :] * qh, axis=-1, keepdims=True)
            p = jnp.exp(s - jnp.max(s, axis=0, keepdims=True))
            o = jnp.sum(p * v_ref[sq, :, h, :], axis=0, keepdims=True) / jnp.sum(p, axis=0, keepdims=True)
            o_ref[sq:sq + 1, sl] = o


def _mem_attend_token(qm, mem_k, mem_v):
    db = qm.shape[0]
    seqs = _row_tile(db, SUBLANES)
    kv = pl.BlockSpec((seqs,) + mem_k.shape[1:], lambda i: (i, 0, 0, 0))
    return pl.pallas_call(
        functools.partial(_mem_attend_token_kernel, seqs=seqs),
        grid=(db // seqs,),
        in_specs=[pl.BlockSpec((seqs, D_MEM), lambda i: (i, 0)), kv, kv],
        out_specs=pl.BlockSpec((seqs, D_MEM), lambda i: (i, 0)),
        out_shape=jax.ShapeDtypeStruct((db, D_MEM), F32),
        compiler_params=_params("arbitrary"),
        name="mem_attend_token",
    )(qm, mem_k, mem_v)


def _mem_kv_kernel(x_ref, w_ref, g_ref, k_ref, v_ref):
    kv = _dot(x_ref[...].astype(BF16), w_ref[...])
    for h in range(MEM_HEADS):
        sl = slice(h * HEAD_DIM, (h + 1) * HEAD_DIM)
        k_ref[:, sl] = _rms(kv[:, sl], g_ref[...])
    v_ref[...] = kv[:, D_MEM:]


def _mem_kv(mem2d, w_b, g, *, tm):
    n = mem2d.shape[0]
    out = jax.ShapeDtypeStruct((n, D_MEM), F32)
    return pl.pallas_call(
        _mem_kv_kernel,
        grid=(n // tm,),
        in_specs=[pl.BlockSpec((tm, D_MODEL), lambda i: (i, 0)),
                  pl.BlockSpec((D_MODEL, 2 * D_MEM), lambda i: (0, 0)),
                  pl.BlockSpec((1, HEAD_DIM), lambda i: (0, 0))],
        out_specs=[pl.BlockSpec((tm, D_MEM), lambda i: (i, 0))] * 2,
        out_shape=[out, out],
        compiler_params=_params("arbitrary"),
        name="mem_kv",
    )(mem2d, w_b, g)


def _merge_kernel(x_ref, hb_ref, yp_ref, ya_ref, ym_ref, wg0_ref, wg1_ref, wg2_ref,
                  wbp_ref, wba_ref, wbm_ref, wo_ref, *rest):
    n_cast = (len(rest) - 1) // 2
    o_ref = rest[n_cast]

    @pl.when(pl.program_id(1) == 0)
    def _():
        o_ref[...] = x_ref[...]

    hb = hb_ref[...]
    merged = (jax.nn.sigmoid(_dot(hb, wg0_ref[...])) * _dot(yp_ref[...].astype(BF16), wbp_ref[...])
              + jax.nn.sigmoid(_dot(hb, wg1_ref[...])) * _dot(ya_ref[...].astype(BF16), wba_ref[...])
              + jax.nn.sigmoid(_dot(hb, wg2_ref[...])) * _dot(ym_ref[...].astype(BF16), wbm_ref[...]))
    o_ref[...] += _dot(merged.astype(BF16), wo_ref[...])
    for src_ref, dst_ref in zip(rest[:n_cast], rest[n_cast + 1:]):
        dst_ref[...] = src_ref[...].astype(BF16)


def _slab_rows(rows, n_steps):
    need = -(-rows // n_steps)
    for r in range(BF16_ROWS, rows + 1, BF16_ROWS):
        if r >= need and rows % r == 0:
            return r
    return rows


def _merge(x2d, hb, yp, ya, ym, w_in_b, wbp_b, wba_b, wbm_b, wo_b, *, tm, cast=()):
    n = x2d.shape[0]
    tc = MERGE_CHUNK
    n_chunks = D_MODEL // tc
    gate0 = D_QKV // tc
    row = lambda i, c: (i, 0)

    def gate_spec(branch):
        return pl.BlockSpec((D_MODEL, tc), lambda i, c: (0, gate0 + branch * n_chunks + c))

    def slab_spec(w):
        r = _slab_rows(w.shape[0], (n // tm) * n_chunks)
        last = w.shape[0] // r - 1
        return pl.BlockSpec((r, w.shape[1]), lambda i, c: (jnp.minimum(i * n_chunks + c, last), 0))

    col = lambda rows: pl.BlockSpec((rows, tc), lambda i, c: (0, c))
    slabs = [slab_spec(w) for w in cast]
    out = pl.pallas_call(
        _merge_kernel,
        grid=(n // tm, n_chunks),
        in_specs=[pl.BlockSpec((tm, D_MODEL), row), pl.BlockSpec((tm, D_MODEL), row),
                  pl.BlockSpec((tm, D_POOL), row), pl.BlockSpec((tm, D_ATTN), row), pl.BlockSpec((tm, D_MEM), row),
                  gate_spec(0), gate_spec(1), gate_spec(2),
                  col(D_POOL), col(D_ATTN), col(D_MEM),
                  pl.BlockSpec((tc, D_MODEL), lambda i, c: (c, 0))] + slabs,
        out_specs=[pl.BlockSpec((tm, D_MODEL), row)] + slabs,
        out_shape=[jax.ShapeDtypeStruct((n, D_MODEL), F32)] + [jax.ShapeDtypeStruct(w.shape, BF16) for w in cast],
        compiler_params=_params("arbitrary", "arbitrary"),
        name="merge",
    )(x2d, hb, yp, ya, ym, w_in_b, w_in_b, w_in_b, wbp_b, wba_b, wbm_b, wo_b, *cast)
    return out if cast else out[0]


def _ffn_act(ua, ug, a_m1, a_m2, g_m1, g_m2, cwa_ref, cwg_ref, cba_ref, cbg_ref):
    ca = cba_ref[...] + cwa_ref[0:1, :] * a_m2 + cwa_ref[1:2, :] * a_m1 + cwa_ref[2:3, :] * ua
    cg = cbg_ref[...] + cwg_ref[0:1, :] * g_m2 + cwg_ref[1:2, :] * g_m1 + cwg_ref[2:3, :] * ug
    return (jax.nn.silu(ca) * cg).astype(BF16)


def _side_page_copy(pt_ref, cache_ref, buf_ref, sem_ref, group, k, *, total_pages, pages_per_step):
    idx = jnp.minimum(group * pages_per_step + k, total_pages - 1)
    page = pt_ref[idx]
    half = lax.rem(group, 2)
    return pltpu.make_async_copy(cache_ref.at[0, page], buf_ref.at[half, k], sem_ref.at[half, k])


def _ffn_seq_kernel(pt_ref, xu_ref, g2_ref, wua_ref, wug_ref, cwa_ref, cwg_ref, cba_ref, cbg_ref, wd_ref, cache_ref,
                    o_ref, sta_ref, stg_ref, psum_ref,
                    hb_ref, exta_ref, extg_ref, carry_ref, pbuf_ref, psem_ref,
                    *, tm, tiles_per_seq, n_chunks, n_steps, total_pages, pages_per_step, n_groups):
    t = pl.program_id(0)
    tu = jnp.minimum(t, n_steps - 1)
    i, c = tu // n_chunks, tu % n_chunks
    n_prev = CONV_WIDTH - 1
    lo = CONV_HALO - n_prev
    page_copy = functools.partial(_side_page_copy, pt_ref, cache_ref, pbuf_ref, psem_ref,
                                  total_pages=total_pages, pages_per_step=pages_per_step)

    @pl.when(t == 0)
    def _():
        exta_ref[1] = jnp.zeros(exta_ref.shape[1:], F32)
        extg_ref[1] = jnp.zeros(extg_ref.shape[1:], F32)
        pbuf_ref[...] = jnp.zeros(pbuf_ref.shape, F32)
        for k in range(pages_per_step):
            page_copy(0, k).start(priority=PAGE_DMA_PRIORITY)

    @pl.when(t + 1 < n_groups)
    def _():
        for k in range(pages_per_step):
            page_copy(t + 1, k).start(priority=PAGE_DMA_PRIORITY)

    @pl.when(t < n_groups)
    def _():
        for k in range(pages_per_step):
            page_copy(t, k).wait()

    @pl.when(c == 0)
    def _():
        hb_ref[...] = _rms(xu_ref[...], g2_ref[...]).astype(BF16)

    @pl.when((t == 0) | (t % n_chunks == 1))
    def _():
        o_ref[...] = xu_ref[...]

    @pl.when(i % tiles_per_seq == 0)
    def _():
        carry_ref[c] = jnp.zeros(carry_ref.shape[1:], F32)

    cur, prv = t % 2, (t + 1) % 2
    hb = hb_ref[...]
    ua = _dot(hb, wua_ref[...])
    ug = _dot(hb, wug_ref[...])
    exta_ref[cur, lo:CONV_HALO, :] = carry_ref[c, 0:n_prev, :]
    extg_ref[cur, lo:CONV_HALO, :] = carry_ref[c, n_prev:2 * n_prev, :]
    exta_ref[cur, CONV_HALO:CONV_HALO + tm, :] = ua
    extg_ref[cur, CONV_HALO:CONV_HALO + tm, :] = ug
    a_last = ua[tm - n_prev:tm, :]
    g_last = ug[tm - n_prev:tm, :]
    carry_ref[c, 0:n_prev, :] = a_last
    carry_ref[c, n_prev:2 * n_prev, :] = g_last

    rows = lambda ref, back: ref[prv, CONV_HALO - back:CONV_HALO - back + tm, :]
    act = _ffn_act(rows(exta_ref, 0), rows(extg_ref, 0), rows(exta_ref, 1), rows(exta_ref, 2),
                   rows(extg_ref, 1), rows(extg_ref, 2), cwa_ref, cwg_ref, cba_ref, cbg_ref)
    o_ref[...] += _dot(act, wd_ref[...])

    for k in range(pages_per_step):
        psum_ref[k] = jnp.sum(pbuf_ref[cur, k], axis=1)

    @pl.when(i % tiles_per_seq == tiles_per_seq - 1)
    def _():
        sta_ref[i // tiles_per_seq, c] = a_last
        stg_ref[i // tiles_per_seq, c] = g_last


def _ffn_step_kernel(x_ref, g2_ref, wua_ref, wug_ref, cwa_ref, cwg_ref, cba_ref, cbg_ref, wd_ref,
                     am2_ref, gm2_ref, am1_ref, gm1_ref,
                     o_ref, ua_ref, ug_ref, hb_ref, acc_ref):
    c = pl.program_id(1)

    @pl.when(c == 0)
    def _():
        hb_ref[...] = _rms(x_ref[...], g2_ref[...]).astype(BF16)
        acc_ref[...] = jnp.zeros(acc_ref.shape, F32)

    hb = hb_ref[...]
    ua = _dot(hb, wua_ref[...])
    ug = _dot(hb, wug_ref[...])
    ua_ref[...] = ua
    ug_ref[...] = ug
    act = _ffn_act(ua, ug, am1_ref[...], am2_ref[...], gm1_ref[...], gm2_ref[...], cwa_ref, cwg_ref, cba_ref, cbg_ref)
    acc_ref[...] += _dot(act, wd_ref[...])

    @pl.when(c == pl.num_programs(1) - 1)
    def _():
        o_ref[...] = x_ref[...] + acc_ref[...]


def _ffn_seq(x2d, g2, w_up_b, conv_w, conv_b2, w_down_b, page_table, cache_k, *, tm, batch, seq):
    n = x2d.shape[0]
    tc = FFN_CHUNK
    n_chunks = D_FF // tc
    assert n_chunks >= 2
    tiles_per_seq = seq // tm
    n_prev = CONV_WIDTH - 1
    n_steps = (n // tm) * n_chunks
    db, n_pages = page_table.shape
    total_pages = db * n_pages
    pages_per_step = -(-total_pages // n_steps)
    n_groups = -(-total_pages // pages_per_step)
    up_tile = lambda t: jnp.minimum(t, n_steps - 1) // n_chunks
    up_chunk = lambda t: jnp.minimum(t, n_steps - 1) % n_chunks
    down_tile = lambda t: jnp.maximum(t - 1, 0) // n_chunks
    down_chunk = lambda t: jnp.maximum(t - 1, 0) % n_chunks
    a_up = lambda rows: pl.BlockSpec((rows, tc), lambda t, pt: (0, up_chunk(t)))
    g_up = lambda rows: pl.BlockSpec((rows, tc), lambda t, pt: (0, n_chunks + up_chunk(t)))
    a_dn = lambda rows: pl.BlockSpec((rows, tc), lambda t, pt: (0, down_chunk(t)))
    g_dn = lambda rows: pl.BlockSpec((rows, tc), lambda t, pt: (0, n_chunks + down_chunk(t)))
    st_spec = pl.BlockSpec((batch, n_chunks, n_prev, tc), lambda t, pt: (0, 0, 0, 0))
    st_shape = jax.ShapeDtypeStruct((batch, n_chunks, n_prev, tc), F32)
    ps_spec = pl.BlockSpec((pages_per_step, N_HEADS, HEAD_DIM), lambda t, pt: (jnp.minimum(t, n_groups), 0, 0))
    ps_shape = jax.ShapeDtypeStruct(((n_groups + 1) * pages_per_step, N_HEADS, HEAD_DIM), F32)
    y, sta, stg, psums = pl.pallas_call(
        functools.partial(_ffn_seq_kernel, tm=tm, tiles_per_seq=tiles_per_seq, n_chunks=n_chunks, n_steps=n_steps,
                          total_pages=total_pages, pages_per_step=pages_per_step, n_groups=n_groups),
        grid_spec=pltpu.PrefetchScalarGridSpec(
            num_scalar_prefetch=1,
            grid=(n_steps + 1,),
            in_specs=[pl.BlockSpec((tm, D_MODEL), lambda t, pt: (up_tile(t), 0)),
                      pl.BlockSpec((1, D_MODEL), lambda t, pt: (0, 0)),
                      a_up(D_MODEL), g_up(D_MODEL), a_dn(CONV_WIDTH), g_dn(CONV_WIDTH), a_dn(1), g_dn(1),
                      pl.BlockSpec((tc, D_MODEL), lambda t, pt: (down_chunk(t), 0)),
                      pl.BlockSpec(memory_space=pl.ANY)],
            out_specs=[pl.BlockSpec((tm, D_MODEL), lambda t, pt: (down_tile(t), 0)), st_spec, st_spec, ps_spec],
            scratch_shapes=[pltpu.VMEM((tm, D_MODEL), BF16),
                            pltpu.VMEM((2, tm + CONV_HALO, tc), F32), pltpu.VMEM((2, tm + CONV_HALO, tc), F32),
                            pltpu.VMEM((n_chunks, SUBLANES, tc), F32),
                            pltpu.VMEM((2, pages_per_step, N_HEADS, PAGE_SIZE, HEAD_DIM), F32),
                            pltpu.SemaphoreType.DMA((2, pages_per_step))],
        ),
        out_shape=[jax.ShapeDtypeStruct((n, D_MODEL), F32), st_shape, st_shape, ps_shape],
        compiler_params=_params("arbitrary"),
        name="ffn_seq",
    )(page_table.reshape(total_pages), x2d, g2, w_up_b, w_up_b, conv_w, conv_w, conv_b2, conv_b2, w_down_b, cache_k)
    unchunk = lambda st: st.transpose(0, 2, 1, 3).reshape(batch, n_prev, D_FF)
    page_sums = psums[:total_pages].reshape(db, n_pages, N_HEADS, HEAD_DIM).transpose(0, 2, 1, 3)
    return y, unchunk(sta), unchunk(stg), page_sums


def _ffn_step(x2d, g2, w_up_b, conv_w, conv_b2, w_down_b, prev2, prev1, *, tm):
    n = x2d.shape[0]
    tc = FFN_CHUNK
    n_chunks = D_FF // tc
    row = lambda i, c: (i, 0)
    a_col = lambda rows: pl.BlockSpec((rows, tc), lambda i, c: (0, c))
    g_col = lambda rows: pl.BlockSpec((rows, tc), lambda i, c: (0, n_chunks + c))
    rows_a = pl.BlockSpec((tm, tc), lambda i, c: (i, c))
    rows_g = pl.BlockSpec((tm, tc), lambda i, c: (i, n_chunks + c))
    up_shape = jax.ShapeDtypeStruct((n, D_FF), F32)
    return pl.pallas_call(
        _ffn_step_kernel,
        grid=(n // tm, n_chunks),
        in_specs=[pl.BlockSpec((tm, D_MODEL), row), pl.BlockSpec((1, D_MODEL), lambda i, c: (0, 0)),
                  a_col(D_MODEL), g_col(D_MODEL), a_col(CONV_WIDTH), g_col(CONV_WIDTH), a_col(1), g_col(1),
                  pl.BlockSpec((tc, D_MODEL), lambda i, c: (c, 0)),
                  rows_a, rows_g, rows_a, rows_g],
        out_specs=[pl.BlockSpec((tm, D_MODEL), row), rows_a, rows_a],
        out_shape=[jax.ShapeDtypeStruct((n, D_MODEL), F32), up_shape, up_shape],
        scratch_shapes=[pltpu.VMEM((tm, D_MODEL), BF16), pltpu.VMEM((tm, D_MODEL), F32)],
        compiler_params=_params("arbitrary", "arbitrary"),
        name="ffn_step",
    )(x2d, g2, w_up_b, w_up_b, conv_w, conv_w, conv_b2, conv_b2, w_down_b, prev2, prev2, prev1, prev1)


def _block_gate_kernel(q_ref, ps_ref, o_ref, km_ref, *, n_blocks, seqs):
    km_ref[...] = jnp.zeros(km_ref.shape, F32)
    lane = lax.broadcasted_iota(jnp.int32, (N_HEADS, LANES), 1)
    row = lax.broadcasted_iota(jnp.int32, (N_HEADS, LANES), 0)

    def one_seq(sq, carry):
        gs_all = jnp.full((N_HEADS, LANES), NEG, F32)
        for h in range(N_HEADS):
            even = ps_ref[sq, h, pl.ds(0, n_blocks, stride=PAGES_PER_BLOCK), :]
            odd = ps_ref[sq, h, pl.ds(1, n_blocks, stride=PAGES_PER_BLOCK), :]
            km_ref[h, 0:n_blocks, :] = (even + odd) * (1.0 / MOBA_BLOCK)
            qh = jnp.broadcast_to(q_ref[sq, :, h * HEAD_DIM:(h + 1) * HEAD_DIM], (N_HEADS, HEAD_DIM))
            gs = _dot_t(qh.astype(BF16), km_ref[h].astype(BF16))
            gs_all = jnp.where(row == h, gs, gs_all)
        gs_all = jnp.where(lane < n_blocks, gs_all, NEG)
        i0, i1, i2 = _top3(gs_all, n_blocks)
        o_ref[sq] = jnp.where(lane == 0, i0, jnp.where(lane == 1, i1, jnp.where(lane == 2, i2, 0)))
        return carry

    lax.fori_loop(0, seqs, one_seq, 0)


def _block_gate(q3, page_sums):
    db, _, n_pages, _ = page_sums.shape
    n_blocks = n_pages // PAGES_PER_BLOCK
    assert PAGES_PER_BLOCK == 2 and MOBA_TOPK <= n_blocks <= LANES and N_HEADS == SUBLANES
    seqs = _row_tile(db, SUBLANES)
    return pl.pallas_call(
        functools.partial(_block_gate_kernel, n_blocks=n_blocks, seqs=seqs),
        grid=(db // seqs,),
        in_specs=[pl.BlockSpec((seqs, 1, D_ATTN), lambda b: (b, 0, 0)),
                  pl.BlockSpec((seqs, N_HEADS, n_pages, HEAD_DIM), lambda b: (b, 0, 0, 0))],
        out_specs=pl.BlockSpec((seqs, N_HEADS, LANES), lambda b: (b, 0, 0)),
        out_shape=jax.ShapeDtypeStruct((db, N_HEADS, LANES), jnp.int32),
        scratch_shapes=[pltpu.VMEM((N_HEADS, LANES, HEAD_DIM), F32)],
        compiler_params=_params("arbitrary"),
        name="block_gate",
    )(q3, page_sums)


N_SEL_PAGES = MOBA_TOPK * PAGES_PER_BLOCK


def _sel_copy(pt_ref, sel_ref, cache_ref, buf_ref, sem_ref, b, h, s):
    blk = sel_ref[b, h * MOBA_TOPK + s // PAGES_PER_BLOCK]
    page = pt_ref[b, blk * PAGES_PER_BLOCK + s % PAGES_PER_BLOCK]
    half = lax.rem(b, 2)
    return pltpu.make_async_copy(cache_ref.at[0, page, h], buf_ref.at[half, h, s], sem_ref.at[half, h])


def _moba_sample_kernel(pt_ref, sel_ref, q_ref, kn_ref, vn_ref, ck_ref, cv_ref, o_ref,
                        kbuf_ref, vbuf_ref, ksem_ref, vsem_ref):
    b = pl.program_id(0)

    def fetch(seq):
        for h in range(N_HEADS):
            for s in range(N_SEL_PAGES):
                _sel_copy(pt_ref, sel_ref, ck_ref, kbuf_ref, ksem_ref, seq, h, s).start()
                _sel_copy(pt_ref, sel_ref, cv_ref, vbuf_ref, vsem_ref, seq, h, s).start()

    @pl.when(b == 0)
    def _():
        fetch(b)

    @pl.when(b + 1 < pl.num_programs(0))
    def _():
        fetch(b + 1)

    half = lax.rem(b, 2)
    for h in range(N_HEADS):
        for s in range(N_SEL_PAGES):
            _sel_copy(pt_ref, sel_ref, ck_ref, kbuf_ref, ksem_ref, b, h, s).wait()
            _sel_copy(pt_ref, sel_ref, cv_ref, vbuf_ref, vsem_ref, b, h, s).wait()
        sl = slice(h * HEAD_DIM, (h + 1) * HEAD_DIM)
        qs = q_ref[0, :, sl] * SCALE
        kk = kbuf_ref[half, h].reshape(N_SEL_PAGES * PAGE_SIZE, HEAD_DIM)
        vv = vbuf_ref[half, h].reshape(N_SEL_PAGES * PAGE_SIZE, HEAD_DIM)
        s_sel = jnp.sum(kk * qs, axis=-1, keepdims=True)
        s_own = jnp.sum(qs * kn_ref[0, :, sl], axis=-1, keepdims=True)
        m = jnp.maximum(jnp.max(s_sel, axis=0, keepdims=True), s_own)
        p = jnp.exp(s_sel - m)
        p_own = jnp.exp(s_own - m)
        l = jnp.sum(p, axis=0, keepdims=True) + p_own
        o = (jnp.sum(p * vv, axis=0, keepdims=True) + p_own * vn_ref[0, :, sl]) / l
        o_ref[0, :, sl] = o.astype(o_ref.dtype)


def _moba_sample(page_table, sel, q3, k3, v3, cache_k, cache_v):
    db = q3.shape[0]
    tok = pl.BlockSpec((1, 1, D_ATTN), lambda b, pt, sl: (b, 0, 0))
    return pl.pallas_call(
        _moba_sample_kernel,
        grid_spec=pltpu.PrefetchScalarGridSpec(
            num_scalar_prefetch=2,
            grid=(db,),
            in_specs=[tok, tok, tok, pl.BlockSpec(memory_space=pl.ANY), pl.BlockSpec(memory_space=pl.ANY)],
            out_specs=tok,
            scratch_shapes=[pltpu.VMEM((2, N_HEADS, N_SEL_PAGES, PAGE_SIZE, HEAD_DIM), F32),
                            pltpu.VMEM((2, N_HEADS, N_SEL_PAGES, PAGE_SIZE, HEAD_DIM), F32),
                            pltpu.SemaphoreType.DMA((2, N_HEADS)), pltpu.SemaphoreType.DMA((2, N_HEADS))],
        ),
        out_shape=jax.ShapeDtypeStruct((db, 1, D_ATTN), F32),
        compiler_params=_params("arbitrary"),
        name="moba_sample",
    )(page_table, sel, q3, k3, v3, cache_k, cache_v)


def _row_tile(n, pref):
    return pref if n % pref == 0 else n


def kernel(x_prompt, x_sample, mem_prompt, cache_k, cache_v, page_table, cache_mem_k, cache_mem_v, state_pool, state_conv, norm1_g, w_in, q_norm_g, k_norm_g, mem_q_norm_g, mem_k_norm_g, w_mem_kv, pool_w, pool_scale, w_branch_pool, w_branch_attn, w_branch_mem, w_out, norm2_g, w_up, conv_w, conv_b, w_down):
    b, s, _ = x_prompt.shape
    db, t, _ = x_sample.shape
    assert t == 1 and norm1_g.shape[0] == 1 and s % MOBA_BLOCK == 0
    n_pages = page_table.shape[1]
    assert n_pages % PAGES_PER_BLOCK == 0
    m_tok = mem_prompt.shape[1]

    w_in_b = w_in[0].astype(BF16)
    wbp_b, wba_b, wbm_b = (w[0].astype(BF16) for w in (w_branch_pool, w_branch_attn, w_branch_mem))
    wo_b = w_out[0].astype(BF16)
    w_mem_b, pool_w_b = w_mem_kv[0].astype(BF16), pool_w[0].astype(BF16)
    g1, g2 = norm1_g, norm2_g
    conv_b2 = conv_b
    norm_args = (q_norm_g, k_norm_g, mem_q_norm_g, pool_w_b, pool_scale)

    n = b * s
    x2d = x_prompt.reshape(n, D_MODEL)
    tm = _row_tile(s, 512)
    tables_p = _rope_tables(jnp.arange(s, dtype=jnp.int32))
    u_p, q_p, k_p, v_p, qm_p, yp_p, hb_p = _inproj_prompt(x2d, g1, w_in_b, tables_p, *norm_args,
                                                          batch=b, seq=s, tm=tm)
    mk2d, mv2d = _mem_kv(mem_prompt.reshape(b * m_tok, D_MODEL), w_mem_b, mem_k_norm_g, tm=m_tok)
    ya_p = _moba_prompt(q_p, k_p, v_p)
    mk_p = mk2d.reshape(b, m_tok, MEM_HEADS, HEAD_DIM)
    mv_p = mv2d.reshape(b, m_tok, MEM_HEADS, HEAD_DIM)
    ym_p = _mem_attend(qm_p, mk2d.reshape(b, m_tok, D_MEM), mv2d.reshape(b, m_tok, D_MEM),
                       rows_per_seq=s, tq=_row_tile(s, 1024))
    x1_p, w_up_b, w_down_b = _merge(x2d, hb_p, yp_p, ya_p, ym_p, w_in_b, wbp_b, wba_b, wbm_b, wo_b, tm=tm,
                                    cast=(w_up[0], w_down[0]))
    y_p, sta_p, stg_p, sums = _ffn_seq(x1_p, g2, w_up_b, conv_w[0], conv_b2, w_down_b, page_table, cache_k,
                                       tm=tm, batch=b, seq=s)

    xs2d = x_sample.reshape(db, D_MODEL)
    past_len = n_pages * PAGE_SIZE
    pos_s = jnp.full((db,), past_len, jnp.int32)
    u_s, q_s, k_s, v_s, qm_s, yp_s, hb_s = _inproj_sample(xs2d, g1, w_in_b, _rope_tables(pos_s), *norm_args,
                                                          state_pool[0], past_len=past_len)
    q3, k3, v3 = (a.reshape(db, 1, D_ATTN) for a in (q_s, k_s, v_s))
    sel = _block_gate(q3, sums)[:, :, :MOBA_TOPK].reshape(db, N_HEADS * MOBA_TOPK)
    ya_s = _moba_sample(page_table, sel, q3, k3, v3, cache_k, cache_v).reshape(db, D_ATTN)
    ym_s = _mem_attend_token(qm_s, cache_mem_k[0], cache_mem_v[0])
    x1_s = _merge(xs2d, hb_s, yp_s, ya_s, ym_s, w_in_b, wbp_b, wba_b, wbm_b, wo_b, tm=db)
    y_s, ua_s, ug_s = _ffn_step(x1_s, g2, w_up_b, conv_w[0], conv_b2, w_down_b,
                                state_conv[0, :, 0], state_conv[0, :, 1], tm=db)

    up_s = jnp.concatenate([ua_s, ug_s], axis=-1)
    return (
        y_p.reshape(b, s, D_MODEL),
        y_s.reshape(db, 1, D_MODEL),
        k_p[None], v_p[None],
        k_s.reshape(1, db, N_HEADS, 1, HEAD_DIM), v_s.reshape(1, db, N_HEADS, 1, HEAD_DIM),
        mk_p[None], mv_p[None],
        u_p.reshape(b, s, D_POOL)[None, :, s - POOL_STATE:],
        jnp.concatenate([state_pool[0][:, 1:], u_s[:, None]], axis=1)[None],
        jnp.concatenate([sta_p, stg_p], axis=-1)[None],
        jnp.concatenate([state_conv[0][:, 1:], up_s[:, None]], axis=1)[None],
    )
```

```python
import functools

import jax
import jax.numpy as jnp
import numpy as np
from jax import lax
from jax.experimental import pallas as pl
from jax.experimental.pallas import tpu as pltpu

F32 = jnp.float32
BF16 = jnp.bfloat16

D_MODEL = 2048
HEAD_DIM = 128
N_HEADS = 8
MEM_HEADS = 4
POOL_WINDOWS = (2, 4, 8, 16)
POOL_GROUP_DIM = 128
D_POOL = len(POOL_WINDOWS) * POOL_GROUP_DIM
POOL_STATE = max(POOL_WINDOWS) - 1
D_ATTN = N_HEADS * HEAD_DIM
D_MEM = MEM_HEADS * HEAD_DIM
D_QKV = D_POOL + 3 * D_ATTN + D_MEM
N_BRANCH = 3
MOBA_BLOCK = 256
MOBA_TOPK = 3
PAGE_SIZE = 128
PAGES_PER_BLOCK = MOBA_BLOCK // PAGE_SIZE
ROPE_THETA = 500000.0
ROPE_DIM = HEAD_DIM // 4
ROPE_HALF = ROPE_DIM // 2
D_FF = 5632
CONV_WIDTH = 3
EPS = 1e-6
NEG = -1e30
SCALE = HEAD_DIM ** -0.5
LOG2E = 1.4426950408889634

LANES = 128
SUBLANES = 8
BF16_ROWS = 2 * SUBLANES
POOL_HALO = 16
CONV_HALO = SUBLANES
VMEM_LIMIT = 56 * 1024 * 1024
FFN_CHUNK = 512
MERGE_CHUNK = 512
PAGE_DMA_PRIORITY = 1


def _params(*sem):
    return pltpu.CompilerParams(dimension_semantics=sem, vmem_limit_bytes=VMEM_LIMIT)


def _rms(x, g):
    ms = jnp.mean(x * x, axis=-1, keepdims=True)
    return x * lax.rsqrt(ms + EPS) * g


def _rope(t, cos, sin_lo, sin_hi):
    return (t * cos + pltpu.roll(t, ROPE_HALF, 1) * sin_hi
            + pltpu.roll(t, HEAD_DIM - ROPE_HALF, 1) * sin_lo)


def _rope_tables(pos):
    half = ROPE_HALF
    inv = ROPE_THETA ** (-jnp.arange(half, dtype=F32) * (2.0 / ROPE_DIM))
    ang = pos.astype(F32)[:, None] * inv[None, :]
    cos, sin = jnp.cos(ang), jnp.sin(ang)
    n = pos.shape[0]
    pad = jnp.zeros((n, HEAD_DIM - ROPE_DIM), F32)
    zero = jnp.zeros((n, half), F32)
    cos_t = jnp.concatenate([cos, cos, pad + 1.0], axis=1)
    sin_lo = jnp.concatenate([-sin, zero, pad], axis=1)
    sin_hi = jnp.concatenate([zero, sin, pad], axis=1)
    return cos_t, sin_lo, sin_hi


def _dot(a, b):
    return jnp.dot(a, b, preferred_element_type=F32)


def _dot_t(a, b):
    return lax.dot_general(a, b, (((1,), (1,)), ((), ())), preferred_element_type=F32)


def _top3(gs, n_valid):
    lane = lax.broadcasted_iota(jnp.int32, gs.shape, 1)
    out = []
    for r in range(MOBA_TOPK):
        mx = jnp.max(gs, axis=-1, keepdims=True)
        idx = jnp.min(jnp.where(gs == mx, lane, LANES), axis=-1, keepdims=True)
        out.append(jnp.where(r < n_valid, idx, -1))
        gs = jnp.where(lane == idx, -jnp.inf, gs)
    return out


def _pool_branch(u, win_sum, cnt, pw_ref, pscale_ref, yp_ref):
    for g, w in enumerate(POOL_WINDOWS):
        sl = slice(g * POOL_GROUP_DIM, (g + 1) * POOL_GROUP_DIM)
        p = win_sum(g, w) / cnt(w) - u[:, sl]
        y = _dot(p.astype(BF16), pw_ref[g]) * pscale_ref[:, sl]
        yp_ref[:, sl] = y.astype(yp_ref.dtype)


def _inproj_prompt_kernel(x_ref, g1_ref, w_ref, cos_ref, slo_ref, shi_ref, qg_ref, kg_ref, mqg_ref,
                          pw_ref, pscale_ref,
                          u_ref, q_ref, k_ref, v_ref, qm_ref, yp_ref, hb_ref,
                          uext_ref, *, tm, tiles_per_seq):
    i = pl.program_id(0)
    t_in_seq = i % tiles_per_seq
    hb = _rms(x_ref[...], g1_ref[...]).astype(BF16)
    hb_ref[...] = hb

    u = _dot(hb, w_ref[:, 0:D_POOL])
    u_ref[...] = u

    @pl.when(t_in_seq == 0)
    def _():
        uext_ref[0:POOL_HALO, :] = jnp.zeros((POOL_HALO, D_POOL), F32)

    uext_ref[POOL_HALO:POOL_HALO + tm, :] = u
    pos = t_in_seq * tm + lax.broadcasted_iota(jnp.int32, (tm, 1), 0)

    def win_sum(g, w):
        sl = slice(g * POOL_GROUP_DIM, (g + 1) * POOL_GROUP_DIM)
        s = uext_ref[POOL_HALO:POOL_HALO + tm, sl]
        for d in range(1, w):
            s = s + uext_ref[POOL_HALO - d:POOL_HALO - d + tm, sl]
        return s

    cos, slo, shi = cos_ref[...], slo_ref[...], shi_ref[...]
    c0 = D_POOL
    qf = _dot(hb, w_ref[:, c0:c0 + D_ATTN])
    for h in range(N_HEADS):
        t = _rms(qf[:, h * HEAD_DIM:(h + 1) * HEAD_DIM], qg_ref[...])
        q_ref[0, h] = _rope(t, cos, slo, shi)
    c0 += D_ATTN
    kf = _dot(hb, w_ref[:, c0:c0 + D_ATTN])
    for h in range(N_HEADS):
        t = _rms(kf[:, h * HEAD_DIM:(h + 1) * HEAD_DIM], kg_ref[...])
        k_ref[0, h] = _rope(t, cos, slo, shi)
    c0 += 2 * D_ATTN
    mf = _dot(hb, w_ref[:, c0:c0 + D_MEM])
    for h in range(MEM_HEADS):
        sl = slice(h * HEAD_DIM, (h + 1) * HEAD_DIM)
        qm_ref[:, sl] = _rms(mf[:, sl], mqg_ref[...]).astype(qm_ref.dtype)
    c0 -= D_ATTN
    vf = _dot(hb, w_ref[:, c0:c0 + D_ATTN])
    for h in range(N_HEADS):
        v_ref[0, h] = vf[:, h * HEAD_DIM:(h + 1) * HEAD_DIM]
    _pool_branch(u, win_sum, lambda w: jnp.minimum(w, pos + 1).astype(F32), pw_ref, pscale_ref, yp_ref)
    uext_ref[0:POOL_HALO, :] = uext_ref[tm:tm + POOL_HALO, :]


def _inproj_prompt(x2d, g1, w_in_b, tables, qg, kg, mqg, pool_w_b, pool_scale, *, batch, seq, tm):
    n = x2d.shape[0]
    tiles_per_seq = seq // tm
    row = lambda i: (i, 0)
    const = lambda i: (0, 0)
    tab = pl.BlockSpec((tm, HEAD_DIM), lambda i: (i % tiles_per_seq, 0))
    head_major = pl.BlockSpec((1, N_HEADS, tm, HEAD_DIM), lambda i: (i // tiles_per_seq, 0, i % tiles_per_seq, 0))
    hm_shape = jax.ShapeDtypeStruct((batch, N_HEADS, seq, HEAD_DIM), F32)
    return pl.pallas_call(
        functools.partial(_inproj_prompt_kernel, tm=tm, tiles_per_seq=tiles_per_seq),
        grid=(n // tm,),
        in_specs=[
            pl.BlockSpec((tm, D_MODEL), row),
            pl.BlockSpec((1, D_MODEL), const),
            pl.BlockSpec((D_MODEL, D_QKV), const, pipeline_mode=pl.Buffered(1)),
            tab, tab, tab,
            pl.BlockSpec((1, HEAD_DIM), const), pl.BlockSpec((1, HEAD_DIM), const), pl.BlockSpec((1, HEAD_DIM), const),
            pl.BlockSpec((len(POOL_WINDOWS), POOL_GROUP_DIM, POOL_GROUP_DIM), lambda i: (0, 0, 0)),
            pl.BlockSpec((1, D_POOL), const),
        ],
        out_specs=[
            pl.BlockSpec((tm, D_POOL), row), head_major, head_major, head_major,
            pl.BlockSpec((tm, D_MEM), row), pl.BlockSpec((tm, D_POOL), row), pl.BlockSpec((tm, D_MODEL), row),
        ],
        out_shape=[
            jax.ShapeDtypeStruct((n, D_POOL), F32), hm_shape, hm_shape, hm_shape,
            jax.ShapeDtypeStruct((n, D_MEM), BF16), jax.ShapeDtypeStruct((n, D_POOL), BF16),
            jax.ShapeDtypeStruct((n, D_MODEL), BF16),
        ],
        scratch_shapes=[pltpu.VMEM((tm + POOL_HALO, D_POOL), F32)],
        compiler_params=_params("arbitrary"),
        name="inproj_prompt",
    )(x2d, g1, w_in_b, *tables, qg, kg, mqg, pool_w_b, pool_scale)


def _inproj_sample_kernel(x_ref, g1_ref, w_ref, cos_ref, slo_ref, shi_ref, qg_ref, kg_ref, mqg_ref,
                          pw_ref, pscale_ref, state_ref,
                          u_ref, q_ref, k_ref, v_ref, qm_ref, yp_ref, hb_ref, *, past_len):
    hb = _rms(x_ref[...], g1_ref[...]).astype(BF16)
    hb_ref[...] = hb
    u = _dot(hb, w_ref[:, 0:D_POOL])
    u_ref[...] = u
    db = state_ref.shape[0]
    srow = lax.broadcasted_iota(jnp.int32, (db, POOL_STATE, POOL_GROUP_DIM), 1)

    def win_sum(g, w):
        sl = slice(g * POOL_GROUP_DIM, (g + 1) * POOL_GROUP_DIM)
        hist = jnp.where(srow >= POOL_STATE - (w - 1), state_ref[:, :, sl], 0.0)
        return u[:, sl] + jnp.sum(hist, axis=1)

    _pool_branch(u, win_sum, lambda w: float(min(w, past_len + 1)), pw_ref, pscale_ref, yp_ref)

    cos, slo, shi = cos_ref[...], slo_ref[...], shi_ref[...]
    c0 = D_POOL
    qf = _dot(hb, w_ref[:, c0:c0 + D_ATTN])
    c0 += D_ATTN
    kf = _dot(hb, w_ref[:, c0:c0 + D_ATTN])
    for h in range(N_HEADS):
        sl = slice(h * HEAD_DIM, (h + 1) * HEAD_DIM)
        q_ref[:, sl] = _rope(_rms(qf[:, sl], qg_ref[...]), cos, slo, shi)
        k_ref[:, sl] = _rope(_rms(kf[:, sl], kg_ref[...]), cos, slo, shi)
    c0 += D_ATTN
    v_ref[...] = _dot(hb, w_ref[:, c0:c0 + D_ATTN])
    c0 += D_ATTN
    mf = _dot(hb, w_ref[:, c0:c0 + D_MEM])
    for h in range(MEM_HEADS):
        sl = slice(h * HEAD_DIM, (h + 1) * HEAD_DIM)
        qm_ref[:, sl] = _rms(mf[:, sl], mqg_ref[...]).astype(qm_ref.dtype)


def _inproj_sample(x2d, g1, w_in_b, tables, qg, kg, mqg, pool_w_b, pool_scale, state_pool, *, past_len):
    db = x2d.shape[0]
    full = lambda shape: pl.BlockSpec(shape, lambda i: (0,) * len(shape))
    return pl.pallas_call(
        functools.partial(_inproj_sample_kernel, past_len=past_len),
        grid=(1,),
        in_specs=[
            full((db, D_MODEL)), full((1, D_MODEL)),
            pl.BlockSpec((D_MODEL, D_QKV), lambda i: (0, 0), pipeline_mode=pl.Buffered(1)),
            full((db, HEAD_DIM)), full((db, HEAD_DIM)), full((db, HEAD_DIM)),
            full((1, HEAD_DIM)), full((1, HEAD_DIM)), full((1, HEAD_DIM)),
            full((len(POOL_WINDOWS), POOL_GROUP_DIM, POOL_GROUP_DIM)), full((1, D_POOL)),
            full((db, POOL_STATE, D_POOL)),
        ],
        out_specs=[full((db, D_POOL)), full((db, D_ATTN)), full((db, D_ATTN)), full((db, D_ATTN)),
                   full((db, D_MEM)), full((db, D_POOL)), full((db, D_MODEL))],
        out_shape=[jax.ShapeDtypeStruct((db, D_POOL), F32), jax.ShapeDtypeStruct((db, D_ATTN), F32),
                   jax.ShapeDtypeStruct((db, D_ATTN), F32), jax.ShapeDtypeStruct((db, D_ATTN), F32),
                   jax.ShapeDtypeStruct((db, D_MEM), F32), jax.ShapeDtypeStruct((db, D_POOL), BF16),
                   jax.ShapeDtypeStruct((db, D_MODEL), BF16)],
        compiler_params=_params("arbitrary"),
        name="inproj_sample",
    )(x2d, g1, w_in_b, *tables, qg, kg, mqg, pool_w_b, pool_scale, state_pool)


def _moba_prompt_kernel(q_ref, k_ref, v_ref, *rest, seq, n_cast):
    o_ref = rest[n_cast]
    qa_ref, ka_ref, vb_ref, km_ref, gs_ref, bt_ref, tri_ref, s_ref, p_ref = rest[2 * n_cast + 1:]
    for src_ref, dst_ref in zip(rest[:n_cast], rest[n_cast + 1:2 * n_cast + 1]):
        dst_ref[...] = src_ref[...].astype(BF16)
    nb = seq // MOBA_BLOCK
    nbp = -(-nb // SUBLANES) * SUBLANES
    shift = MOBA_BLOCK.bit_length() - 1
    q = q_ref[0, 0]
    k = k_ref[0, 0]
    row_blk = lax.shift_right_logical(lax.broadcasted_iota(jnp.int32, (seq, LANES), 0), shift)
    lane = lax.broadcasted_iota(jnp.int32, (seq, LANES), 1)
    ka_ref[:, 0:HEAD_DIM] = k.astype(BF16)
    ka_ref[:, HEAD_DIM:] = jnp.where(lane == row_blk, 1.0, 0.0).astype(BF16)
    vb_ref[:, 0:HEAD_DIM] = v_ref[0, 0].astype(BF16)
    vb_ref[:, HEAD_DIM:] = jnp.ones((seq, HEAD_DIM), BF16)
    qa_ref[:, 0:HEAD_DIM] = (q * (SCALE * LOG2E)).astype(BF16)

    km_ref[...] = jnp.zeros(km_ref.shape, F32)
    km_ref[0:nb, :] = jnp.sum(k.reshape(nb, MOBA_BLOCK, HEAD_DIM), axis=1) * (1.0 / MOBA_BLOCK)
    gst = _dot_t(km_ref[...].astype(BF16), q.astype(BF16))
    jidx = lax.broadcasted_iota(jnp.int32, (nbp, seq), 0)
    qblk = lax.shift_right_logical(lax.broadcasted_iota(jnp.int32, (nbp, seq), 1), shift)
    past = jidx < qblk
    gs = jnp.where(past, gst[0:nbp, :], NEG)
    gs_ref[...] = gs
    rank = jnp.zeros((nbp, seq), F32)
    for jp in range(nb):
        other = gs_ref[jp:jp + 1, :]
        rank = rank + jnp.where(jidx > jp, jnp.where(other >= gs, 1.0, 0.0), jnp.where(other > gs, 1.0, 0.0))
    bt_ref[...] = jnp.zeros(bt_ref.shape, F32)
    bt_ref[0:nbp, :] = jnp.where(past, jnp.where(rank < MOBA_TOPK, 0.0, NEG), jnp.where(jidx == qblk, 0.0, NEG))
    for c in range(seq // MOBA_BLOCK):
        rows = slice(c * MOBA_BLOCK, (c + 1) * MOBA_BLOCK)
        qa_ref[rows, HEAD_DIM:] = bt_ref[:, rows].T.astype(BF16)

    tri_ref[...] = jnp.where(lax.broadcasted_iota(jnp.int32, tri_ref.shape, 1)
                             <= lax.broadcasted_iota(jnp.int32, tri_ref.shape, 0), 0.0, NEG)

    for bq in range(nb):
        par = bq % 2
        n_keys = bq + 1
        rows = slice(bq * MOBA_BLOCK, (bq + 1) * MOBA_BLOCK)
        qa = qa_ref[rows, :]
        m_part = None
        for j in range(n_keys):
            s = _dot_t(qa, ka_ref[j * MOBA_BLOCK:(j + 1) * MOBA_BLOCK, :])
            if j == bq:
                s = s + tri_ref[...]
            s_ref[par, j] = s
            mj = jnp.maximum(s[:, :LANES], s[:, LANES:])
            m_part = mj if m_part is None else jnp.maximum(m_part, mj)
        m = jnp.broadcast_to(jnp.max(m_part, axis=-1, keepdims=True), (MOBA_BLOCK, LANES))
        for j in range(n_keys):
            for half in range(MOBA_BLOCK // LANES):
                cols = slice(half * LANES, (half + 1) * LANES)
                p = jnp.exp2(s_ref[par, j, :, cols] - m)
                p_ref[par, :, j * MOBA_BLOCK + half * LANES:j * MOBA_BLOCK + (half + 1) * LANES] = p.astype(BF16)
        ol = _dot(p_ref[par, :, 0:n_keys * MOBA_BLOCK], vb_ref[0:n_keys * MOBA_BLOCK, :])
        o_ref[rows, :] = (ol[:, 0:HEAD_DIM] / ol[:, HEAD_DIM:]).astype(o_ref.dtype)


def _slab_rows(rows, n_steps):
    need = -(-rows // n_steps)
    for r in range(BF16_ROWS, rows + 1, BF16_ROWS):
        if r >= need and rows % r == 0:
            return r
    return rows


def _moba_prompt(q, k, v, cast=()):
    b, h, s, _ = q.shape
    nb = s // MOBA_BLOCK
    assert nb <= LANES
    nbp = -(-nb // SUBLANES) * SUBLANES
    blk = pl.BlockSpec((1, 1, s, HEAD_DIM), lambda bi, hi: (bi, hi, 0, 0))

    def slab_spec(w):
        r = _slab_rows(w.shape[0], b * h)
        last = w.shape[0] // r - 1
        return pl.BlockSpec((r, w.shape[1]), lambda bi, hi: (jnp.minimum(bi * h + hi, last), 0))

    slabs = [slab_spec(w) for w in cast]
    out = pl.pallas_call(
        functools.partial(_moba_prompt_kernel, seq=s, n_cast=len(cast)),
        grid=(b, h),
        in_specs=[blk, blk, blk] + slabs,
        out_specs=[pl.BlockSpec((s, HEAD_DIM), lambda bi, hi: (bi, hi))] + slabs,
        out_shape=[jax.ShapeDtypeStruct((b * s, h * HEAD_DIM), BF16)]
        + [jax.ShapeDtypeStruct(w.shape, BF16) for w in cast],
        scratch_shapes=[pltpu.VMEM((s, 2 * HEAD_DIM), BF16), pltpu.VMEM((s, 2 * HEAD_DIM), BF16),
                        pltpu.VMEM((s, 2 * HEAD_DIM), BF16),
                        pltpu.VMEM((LANES, HEAD_DIM), F32), pltpu.VMEM((nbp, s), F32), pltpu.VMEM((LANES, s), F32),
                        pltpu.VMEM((MOBA_BLOCK, MOBA_BLOCK), F32),
                        pltpu.VMEM((2, nb, MOBA_BLOCK, MOBA_BLOCK), F32), pltpu.VMEM((2, MOBA_BLOCK, s), BF16)],
        compiler_params=_params("arbitrary", "arbitrary"),
        name="moba_prompt",
    )(q, k, v, *cast)
    return out if cast else out[0]


def _mem_attend_kernel(q_ref, k_ref, v_ref, o_ref):
    for h in range(MEM_HEADS):
        sl = slice(h * HEAD_DIM, (h + 1) * HEAD_DIM)
        qs = (q_ref[:, sl].astype(F32) * SCALE).astype(BF16)
        s = _dot_t(qs, k_ref[0, :, sl].astype(BF16))
        m = jnp.max(s, axis=-1, keepdims=True)
        p = jnp.exp(s - m)
        l = jnp.sum(p, axis=-1, keepdims=True)
        o = _dot(p.astype(BF16), v_ref[0, :, sl].astype(BF16)) / l
        o_ref[:, sl] = o.astype(o_ref.dtype)


def _mem_attend(qm, mem_k, mem_v, *, rows_per_seq, tq):
    n = qm.shape[0]
    m_tok = mem_k.shape[1]
    tiles = rows_per_seq // tq
    kv = pl.BlockSpec((1, m_tok, D_MEM), lambda i: (i // tiles, 0, 0))
    return pl.pallas_call(
        _mem_attend_kernel,
        grid=(n // tq,),
        in_specs=[pl.BlockSpec((tq, D_MEM), lambda i: (i, 0)), kv, kv],
        out_specs=pl.BlockSpec((tq, D_MEM), lambda i: (i, 0)),
        out_shape=jax.ShapeDtypeStruct((n, D_MEM), BF16),
        compiler_params=_params("arbitrary"),
        name="mem_attend",
    )(qm, mem_k, mem_v)


def _mem_attend_token_kernel(q_ref, k_ref, v_ref, o_ref, *, seqs):
    for sq in range(seqs):
        for h in range(MEM_HEADS):
            sl = slice(h * HEAD_DIM, (h + 1) * HEAD_DIM)
            qh = q_ref[sq:sq + 1, sl] * SCALE
            s = jnp.sum(k_ref[sq, :, h, :] * qh, axis=-1, keepdims=True)
            p = jnp.exp(s - jnp.max(s, axis=0, keepdims=True))
            o = jnp.sum(p * v_ref[sq, :, h, :], axis=0, keepdims=True) / jnp.sum(p, axis=0, keepdims=True)
            o_ref[sq:sq + 1, sl] = o


def _mem_attend_token(qm, mem_k, mem_v):
    db = qm.shape[0]
    seqs = _row_tile(db, SUBLANES)
    kv = pl.BlockSpec((seqs,) + mem_k.shape[1:], lambda i: (i, 0, 0, 0))
    return pl.pallas_call(
        functools.partial(_mem_attend_token_kernel, seqs=seqs),
        grid=(db // seqs,),
        in_specs=[pl.BlockSpec((seqs, D_MEM), lambda i: (i, 0)), kv, kv],
        out_specs=pl.BlockSpec((seqs, D_MEM), lambda i: (i, 0)),
        out_shape=jax.ShapeDtypeStruct((db, D_MEM), F32),
        compiler_params=_params("arbitrary"),
        name="mem_attend_token",
    )(qm, mem_k, mem_v)


def _mem_kv_kernel(x_ref, w_ref, g_ref, k_ref, v_ref):
    kv = _dot(x_ref[...].astype(BF16), w_ref[...])
    for h in range(MEM_HEADS):
        sl = slice(h * HEAD_DIM, (h + 1) * HEAD_DIM)
        k_ref[:, sl] = _rms(kv[:, sl], g_ref[...])
    v_ref[...] = kv[:, D_MEM:]


def _mem_kv(mem2d, w_b, g, *, tm):
    n = mem2d.shape[0]
    out = jax.ShapeDtypeStruct((n, D_MEM), F32)
    return pl.pallas_call(
        _mem_kv_kernel,
        grid=(n // tm,),
        in_specs=[pl.BlockSpec((tm, D_MODEL), lambda i: (i, 0)),
                  pl.BlockSpec((D_MODEL, 2 * D_MEM), lambda i: (0, 0)),
                  pl.BlockSpec((1, HEAD_DIM), lambda i: (0, 0))],
        out_specs=[pl.BlockSpec((tm, D_MEM), lambda i: (i, 0))] * 2,
        out_shape=[out, out],
        compiler_params=_params("arbitrary"),
        name="mem_kv",
    )(mem2d, w_b, g)


def _merge_kernel(x_ref, hb_ref, yp_ref, ya_ref, ym_ref, wg0_ref, wg1_ref, wg2_ref,
                  wbp_ref, wba_ref, wbm_ref, wo_ref, o_ref):
    @pl.when(pl.program_id(1) == 0)
    def _():
        o_ref[...] = x_ref[...]

    hb = hb_ref[...]
    merged = (jax.nn.sigmoid(_dot(hb, wg0_ref[...])) * _dot(yp_ref[...].astype(BF16), wbp_ref[...])
              + jax.nn.sigmoid(_dot(hb, wg1_ref[...])) * _dot(ya_ref[...].astype(BF16), wba_ref[...])
              + jax.nn.sigmoid(_dot(hb, wg2_ref[...])) * _dot(ym_ref[...].astype(BF16), wbm_ref[...]))
    o_ref[...] += _dot(merged.astype(BF16), wo_ref[...])


def _merge(x2d, hb, yp, ya, ym, w_in_b, wbp_b, wba_b, wbm_b, wo_b, *, tm):
    n = x2d.shape[0]
    tc = MERGE_CHUNK
    n_chunks = D_MODEL // tc
    gate0 = D_QKV // tc
    row = lambda i, c: (i, 0)

    def gate_spec(branch):
        return pl.BlockSpec((D_MODEL, tc), lambda i, c: (0, gate0 + branch * n_chunks + c))

    col = lambda rows: pl.BlockSpec((rows, tc), lambda i, c: (0, c))
    return pl.pallas_call(
        _merge_kernel,
        grid=(n // tm, n_chunks),
        in_specs=[pl.BlockSpec((tm, D_MODEL), row), pl.BlockSpec((tm, D_MODEL), row),
                  pl.BlockSpec((tm, D_POOL), row), pl.BlockSpec((tm, D_ATTN), row), pl.BlockSpec((tm, D_MEM), row),
                  gate_spec(0), gate_spec(1), gate_spec(2),
                  col(D_POOL), col(D_ATTN), col(D_MEM),
                  pl.BlockSpec((tc, D_MODEL), lambda i, c: (c, 0))],
        out_specs=pl.BlockSpec((tm, D_MODEL), row),
        out_shape=jax.ShapeDtypeStruct((n, D_MODEL), F32),
        compiler_params=_params("arbitrary", "arbitrary"),
        name="merge",
    )(x2d, hb, yp, ya, ym, w_in_b, w_in_b, w_in_b, wbp_b, wba_b, wbm_b, wo_b)


def _ffn_act(ua, ug, a_m1, a_m2, g_m1, g_m2, cwa_ref, cwg_ref, cba_ref, cbg_ref):
    ca = cba_ref[...] + cwa_ref[0:1, :] * a_m2 + cwa_ref[1:2, :] * a_m1 + cwa_ref[2:3, :] * ua
    cg = cbg_ref[...] + cwg_ref[0:1, :] * g_m2 + cwg_ref[1:2, :] * g_m1 + cwg_ref[2:3, :] * ug
    return (jax.nn.silu(ca) * cg).astype(BF16)


def _side_page_copy(pt_ref, cache_ref, buf_ref, sem_ref, group, k, *, total_pages, pages_per_step):
    idx = jnp.minimum(group * pages_per_step + k, total_pages - 1)
    page = pt_ref[idx]
    half = lax.rem(group, 2)
    return pltpu.make_async_copy(cache_ref.at[0, page], buf_ref.at[half, k], sem_ref.at[half, k])


def _ffn_seq_kernel(pt_ref, xu_ref, g2_ref, wua_ref, wug_ref, cwa_ref, cwg_ref, cba_ref, cbg_ref, wd_ref, cache_ref,
                    o_ref, sta_ref, stg_ref, psum_ref,
                    hb_ref, exta_ref, extg_ref, carry_ref, pbuf_ref, psem_ref,
                    *, tm, tiles_per_seq, n_chunks, n_steps, total_pages, pages_per_step, n_groups):
    t = pl.program_id(0)
    tu = jnp.minimum(t, n_steps - 1)
    i, c = tu // n_chunks, tu % n_chunks
    n_prev = CONV_WIDTH - 1
    lo = CONV_HALO - n_prev
    page_copy = functools.partial(_side_page_copy, pt_ref, cache_ref, pbuf_ref, psem_ref,
                                  total_pages=total_pages, pages_per_step=pages_per_step)

    @pl.when(t == 0)
    def _():
        exta_ref[1] = jnp.zeros(exta_ref.shape[1:], F32)
        extg_ref[1] = jnp.zeros(extg_ref.shape[1:], F32)
        pbuf_ref[...] = jnp.zeros(pbuf_ref.shape, F32)
        for k in range(pages_per_step):
            page_copy(0, k).start(priority=PAGE_DMA_PRIORITY)

    @pl.when(t + 1 < n_groups)
    def _():
        for k in range(pages_per_step):
            page_copy(t + 1, k).start(priority=PAGE_DMA_PRIORITY)

    @pl.when(t < n_groups)
    def _():
        for k in range(pages_per_step):
            page_copy(t, k).wait()

    @pl.when(c == 0)
    def _():
        hb_ref[...] = _rms(xu_ref[...], g2_ref[...]).astype(BF16)

    @pl.when((t == 0) | (t % n_chunks == 1))
    def _():
        o_ref[...] = xu_ref[...]

    @pl.when(i % tiles_per_seq == 0)
    def _():
        carry_ref[c] = jnp.zeros(carry_ref.shape[1:], F32)

    cur, prv = t % 2, (t + 1) % 2
    hb = hb_ref[...]
    ua = _dot(hb, wua_ref[...])
    ug = _dot(hb, wug_ref[...])
    exta_ref[cur, lo:CONV_HALO, :] = carry_ref[c, 0:n_prev, :]
    extg_ref[cur, lo:CONV_HALO, :] = carry_ref[c, n_prev:2 * n_prev, :]
    exta_ref[cur, CONV_HALO:CONV_HALO + tm, :] = ua
    extg_ref[cur, CONV_HALO:CONV_HALO + tm, :] = ug
    a_last = ua[tm - n_prev:tm, :]
    g_last = ug[tm - n_prev:tm, :]
    carry_ref[c, 0:n_prev, :] = a_last
    carry_ref[c, n_prev:2 * n_prev, :] = g_last

    rows = lambda ref, back: ref[prv, CONV_HALO - back:CONV_HALO - back + tm, :]
    act = _ffn_act(rows(exta_ref, 0), rows(extg_ref, 0), rows(exta_ref, 1), rows(exta_ref, 2),
                   rows(extg_ref, 1), rows(extg_ref, 2), cwa_ref, cwg_ref, cba_ref, cbg_ref)
    o_ref[...] += _dot(act, wd_ref[...])

    for k in range(pages_per_step):
        psum_ref[k] = jnp.sum(pbuf_ref[cur, k], axis=1)

    @pl.when(i % tiles_per_seq == tiles_per_seq - 1)
    def _():
        sta_ref[i // tiles_per_seq, c] = a_last
        stg_ref[i // tiles_per_seq, c] = g_last


def _ffn_step_kernel(x_ref, g2_ref, wua_ref, wug_ref, cwa_ref, cwg_ref, cba_ref, cbg_ref, wd_ref,
                     am2_ref, gm2_ref, am1_ref, gm1_ref,
                     o_ref, ua_ref, ug_ref, hb_ref, acc_ref):
    c = pl.program_id(1)

    @pl.when(c == 0)
    def _():
        hb_ref[...] = _rms(x_ref[...], g2_ref[...]).astype(BF16)
        acc_ref[...] = jnp.zeros(acc_ref.shape, F32)

    hb = hb_ref[...]
    ua = _dot(hb, wua_ref[...])
    ug = _dot(hb, wug_ref[...])
    ua_ref[...] = ua
    ug_ref[...] = ug
    act = _ffn_act(ua, ug, am1_ref[...], am2_ref[...], gm1_ref[...], gm2_ref[...], cwa_ref, cwg_ref, cba_ref, cbg_ref)
    acc_ref[...] += _dot(act, wd_ref[...])

    @pl.when(c == pl.num_programs(1) - 1)
    def _():
        o_ref[...] = x_ref[...] + acc_ref[...]


def _ffn_seq(x2d, g2, w_up_b, conv_w, conv_b2, w_down_b, page_table, cache_k, *, tm, batch, seq):
    n = x2d.shape[0]
    tc = FFN_CHUNK
    n_chunks = D_FF // tc
    assert n_chunks >= 2
    tiles_per_seq = seq // tm
    n_prev = CONV_WIDTH - 1
    n_steps = (n // tm) * n_chunks
    db, n_pages = page_table.shape
    total_pages = db * n_pages
    pages_per_step = -(-total_pages // n_steps)
    n_groups = -(-total_pages // pages_per_step)
    up_tile = lambda t: jnp.minimum(t, n_steps - 1) // n_chunks
    up_chunk = lambda t: jnp.minimum(t, n_steps - 1) % n_chunks
    down_tile = lambda t: jnp.maximum(t - 1, 0) // n_chunks
    down_chunk = lambda t: jnp.maximum(t - 1, 0) % n_chunks
    a_up = lambda rows: pl.BlockSpec((rows, tc), lambda t, pt: (0, up_chunk(t)))
    g_up = lambda rows: pl.BlockSpec((rows, tc), lambda t, pt: (0, n_chunks + up_chunk(t)))
    a_dn = lambda rows: pl.BlockSpec((rows, tc), lambda t, pt: (0, down_chunk(t)))
    g_dn = lambda rows: pl.BlockSpec((rows, tc), lambda t, pt: (0, n_chunks + down_chunk(t)))
    st_spec = pl.BlockSpec((batch, n_chunks, n_prev, tc), lambda t, pt: (0, 0, 0, 0))
    st_shape = jax.ShapeDtypeStruct((batch, n_chunks, n_prev, tc), F32)
    ps_spec = pl.BlockSpec((pages_per_step, N_HEADS, HEAD_DIM), lambda t, pt: (jnp.minimum(t, n_groups), 0, 0))
    ps_shape = jax.ShapeDtypeStruct(((n_groups + 1) * pages_per_step, N_HEADS, HEAD_DIM), F32)
    y, sta, stg, psums = pl.pallas_call(
        functools.partial(_ffn_seq_kernel, tm=tm, tiles_per_seq=tiles_per_seq, n_chunks=n_chunks, n_steps=n_steps,
                          total_pages=total_pages, pages_per_step=pages_per_step, n_groups=n_groups),
        grid_spec=pltpu.PrefetchScalarGridSpec(
            num_scalar_prefetch=1,
            grid=(n_steps + 1,),
            in_specs=[pl.BlockSpec((tm, D_MODEL), lambda t, pt: (up_tile(t), 0)),
                      pl.BlockSpec((1, D_MODEL), lambda t, pt: (0, 0)),
                      a_up(D_MODEL), g_up(D_MODEL), a_dn(CONV_WIDTH), g_dn(CONV_WIDTH), a_dn(1), g_dn(1),
                      pl.BlockSpec((tc, D_MODEL), lambda t, pt: (down_chunk(t), 0)),
                      pl.BlockSpec(memory_space=pl.ANY)],
            out_specs=[pl.BlockSpec((tm, D_MODEL), lambda t, pt: (down_tile(t), 0)), st_spec, st_spec, ps_spec],
            scratch_shapes=[pltpu.VMEM((tm, D_MODEL), BF16),
                            pltpu.VMEM((2, tm + CONV_HALO, tc), F32), pltpu.VMEM((2, tm + CONV_HALO, tc), F32),
                            pltpu.VMEM((n_chunks, SUBLANES, tc), F32),
                            pltpu.VMEM((2, pages_per_step, N_HEADS, PAGE_SIZE, HEAD_DIM), F32),
                            pltpu.SemaphoreType.DMA((2, pages_per_step))],
        ),
        out_shape=[jax.ShapeDtypeStruct((n, D_MODEL), F32), st_shape, st_shape, ps_shape],
        compiler_params=_params("arbitrary"),
        name="ffn_seq",
    )(page_table.reshape(total_pages), x2d, g2, w_up_b, w_up_b, conv_w, conv_w, conv_b2, conv_b2, w_down_b, cache_k)
    unchunk = lambda st: st.transpose(0, 2, 1, 3).reshape(batch, n_prev, D_FF)
    page_sums = psums[:total_pages].reshape(db, n_pages, N_HEADS, HEAD_DIM).transpose(0, 2, 1, 3)
    return y, unchunk(sta), unchunk(stg), page_sums


def _ffn_step(x2d, g2, w_up_b, conv_w, conv_b2, w_down_b, prev2, prev1, *, tm):
    n = x2d.shape[0]
    tc = FFN_CHUNK
    n_chunks = D_FF // tc
    row = lambda i, c: (i, 0)
    a_col = lambda rows: pl.BlockSpec((rows, tc), lambda i, c: (0, c))
    g_col = lambda rows: pl.BlockSpec((rows, tc), lambda i, c: (0, n_chunks + c))
    rows_a = pl.BlockSpec((tm, tc), lambda i, c: (i, c))
    rows_g = pl.BlockSpec((tm, tc), lambda i, c: (i, n_chunks + c))
    up_shape = jax.ShapeDtypeStruct((n, D_FF), F32)
    return pl.pallas_call(
        _ffn_step_kernel,
        grid=(n // tm, n_chunks),
        in_specs=[pl.BlockSpec((tm, D_MODEL), row), pl.BlockSpec((1, D_MODEL), lambda i, c: (0, 0)),
                  a_col(D_MODEL), g_col(D_MODEL), a_col(CONV_WIDTH), g_col(CONV_WIDTH), a_col(1), g_col(1),
                  pl.BlockSpec((tc, D_MODEL), lambda i, c: (c, 0)),
                  rows_a, rows_g, rows_a, rows_g],
        out_specs=[pl.BlockSpec((tm, D_MODEL), row), rows_a, rows_a],
        out_shape=[jax.ShapeDtypeStruct((n, D_MODEL), F32), up_shape, up_shape],
        scratch_shapes=[pltpu.VMEM((tm, D_MODEL), BF16), pltpu.VMEM((tm, D_MODEL), F32)],
        compiler_params=_params("arbitrary", "arbitrary"),
        name="ffn_step",
    )(x2d, g2, w_up_b, w_up_b, conv_w, conv_w, conv_b2, conv_b2, w_down_b, prev2, prev2, prev1, prev1)


def _block_gate_kernel(q_ref, ps_ref, o_ref, km_ref, *, n_blocks, seqs):
    km_ref[...] = jnp.zeros(km_ref.shape, F32)
    lane = lax.broadcasted_iota(jnp.int32, (N_HEADS, LANES), 1)
    row = lax.broadcasted_iota(jnp.int32, (N_HEADS, LANES), 0)

    def one_seq(sq, carry):
        gs_all = jnp.full((N_HEADS, LANES), NEG, F32)
        for h in range(N_HEADS):
            even = ps_ref[sq, h, pl.ds(0, n_blocks, stride=PAGES_PER_BLOCK), :]
            odd = ps_ref[sq, h, pl.ds(1, n_blocks, stride=PAGES_PER_BLOCK), :]
            km_ref[h, 0:n_blocks, :] = (even + odd) * (1.0 / MOBA_BLOCK)
            qh = jnp.broadcast_to(q_ref[sq, :, h * HEAD_DIM:(h + 1) * HEAD_DIM], (N_HEADS, HEAD_DIM))
            gs = _dot_t(qh.astype(BF16), km_ref[h].astype(BF16))
            gs_all = jnp.where(row == h, gs, gs_all)
        gs_all = jnp.where(lane < n_blocks, gs_all, NEG)
        i0, i1, i2 = _top3(gs_all, n_blocks)
        o_ref[sq] = jnp.where(lane == 0, i0, jnp.where(lane == 1, i1, jnp.where(lane == 2, i2, 0)))
        return carry

    lax.fori_loop(0, seqs, one_seq, 0)


def _block_gate(q3, page_sums):
    db, _, n_pages, _ = page_sums.shape
    n_blocks = n_pages // PAGES_PER_BLOCK
    assert PAGES_PER_BLOCK == 2 and MOBA_TOPK <= n_blocks <= LANES and N_HEADS == SUBLANES
    seqs = _row_tile(db, SUBLANES)
    return pl.pallas_call(
        functools.partial(_block_gate_kernel, n_blocks=n_blocks, seqs=seqs),
        grid=(db // seqs,),
        in_specs=[pl.BlockSpec((seqs, 1, D_ATTN), lambda b: (b, 0, 0)),
                  pl.BlockSpec((seqs, N_HEADS, n_pages, HEAD_DIM), lambda b: (b, 0, 0, 0))],
        out_specs=pl.BlockSpec((seqs, N_HEADS, LANES), lambda b: (b, 0, 0)),
        out_shape=jax.ShapeDtypeStruct((db, N_HEADS, LANES), jnp.int32),
        scratch_shapes=[pltpu.VMEM((N_HEADS, LANES, HEAD_DIM), F32)],
        compiler_params=_params("arbitrary"),
        name="block_gate",
    )(q3, page_sums)


N_SEL_PAGES = MOBA_TOPK * PAGES_PER_BLOCK


def _sel_copy(pt_ref, sel_ref, cache_ref, buf_ref, sem_ref, b, h, s):
    blk = sel_ref[b, h * MOBA_TOPK + s // PAGES_PER_BLOCK]
    page = pt_ref[b, blk * PAGES_PER_BLOCK + s % PAGES_PER_BLOCK]
    half = lax.rem(b, 2)
    return pltpu.make_async_copy(cache_ref.at[0, page, h], buf_ref.at[half, h, s], sem_ref.at[half, h])


def _moba_sample_kernel(pt_ref, sel_ref, q_ref, kn_ref, vn_ref, ck_ref, cv_ref, o_ref,
                        kbuf_ref, vbuf_ref, ksem_ref, vsem_ref):
    b = pl.program_id(0)

    def fetch(seq):
        for h in range(N_HEADS):
            for s in range(N_SEL_PAGES):
                _sel_copy(pt_ref, sel_ref, ck_ref, kbuf_ref, ksem_ref, seq, h, s).start()
                _sel_copy(pt_ref, sel_ref, cv_ref, vbuf_ref, vsem_ref, seq, h, s).start()

    @pl.when(b == 0)
    def _():
        fetch(b)

    @pl.when(b + 1 < pl.num_programs(0))
    def _():
        fetch(b + 1)

    half = lax.rem(b, 2)
    for h in range(N_HEADS):
        for s in range(N_SEL_PAGES):
            _sel_copy(pt_ref, sel_ref, ck_ref, kbuf_ref, ksem_ref, b, h, s).wait()
            _sel_copy(pt_ref, sel_ref, cv_ref, vbuf_ref, vsem_ref, b, h, s).wait()
        sl = slice(h * HEAD_DIM, (h + 1) * HEAD_DIM)
        qs = q_ref[0, :, sl] * SCALE
        kk = kbuf_ref[half, h].reshape(N_SEL_PAGES * PAGE_SIZE, HEAD_DIM)
        vv = vbuf_ref[half, h].reshape(N_SEL_PAGES * PAGE_SIZE, HEAD_DIM)
        s_sel = jnp.sum(kk * qs, axis=-1, keepdims=True)
        s_own = jnp.sum(qs * kn_ref[0, :, sl], axis=-1, keepdims=True)
        m = jnp.maximum(jnp.max(s_sel, axis=0, keepdims=True), s_own)
        p = jnp.exp(s_sel - m)
        p_own = jnp.exp(s_own - m)
        l = jnp.sum(p, axis=0, keepdims=True) + p_own
        o = (jnp.sum(p * vv, axis=0, keepdims=True) + p_own * vn_ref[0, :, sl]) / l
        o_ref[0, :, sl] = o.astype(o_ref.dtype)


def _moba_sample(page_table, sel, q3, k3, v3, cache_k, cache_v):
    db = q3.shape[0]
    tok = pl.BlockSpec((1, 1, D_ATTN), lambda b, pt, sl: (b, 0, 0))
    return pl.pallas_call(
        _moba_sample_kernel,
        grid_spec=pltpu.PrefetchScalarGridSpec(
            num_scalar_prefetch=2,
            grid=(db,),
            in_specs=[tok, tok, tok, pl.BlockSpec(memory_space=pl.ANY), pl.BlockSpec(memory_space=pl.ANY)],
            out_specs=tok,
            scratch_shapes=[pltpu.VMEM((2, N_HEADS, N_SEL_PAGES, PAGE_SIZE, HEAD_DIM), F32),
                            pltpu.VMEM((2, N_HEADS, N_SEL_PAGES, PAGE_SIZE, HEAD_DIM), F32),
                            pltpu.SemaphoreType.DMA((2, N_HEADS)), pltpu.SemaphoreType.DMA((2, N_HEADS))],
        ),
        out_shape=jax.ShapeDtypeStruct((db, 1, D_ATTN), F32),
        compiler_params=_params("arbitrary"),
        name="moba_sample",
    )(page_table, sel, q3, k3, v3, cache_k, cache_v)


def _row_tile(n, pref):
    return pref if n % pref == 0 else n


def kernel(x_prompt, x_sample, mem_prompt, cache_k, cache_v, page_table, cache_mem_k, cache_mem_v, state_pool, state_conv, norm1_g, w_in, q_norm_g, k_norm_g, mem_q_norm_g, mem_k_norm_g, w_mem_kv, pool_w, pool_scale, w_branch_pool, w_branch_attn, w_branch_mem, w_out, norm2_g, w_up, conv_w, conv_b, w_down):
    b, s, _ = x_prompt.shape
    db, t, _ = x_sample.shape
    assert t == 1 and norm1_g.shape[0] == 1 and s % MOBA_BLOCK == 0
    n_pages = page_table.shape[1]
    assert n_pages % PAGES_PER_BLOCK == 0
    m_tok = mem_prompt.shape[1]

    w_in_b = w_in[0].astype(BF16)
    wbp_b, wba_b, wbm_b = (w[0].astype(BF16) for w in (w_branch_pool, w_branch_attn, w_branch_mem))
    wo_b = w_out[0].astype(BF16)
    w_mem_b, pool_w_b = w_mem_kv[0].astype(BF16), pool_w[0].astype(BF16)
    g1, g2 = norm1_g, norm2_g
    conv_b2 = conv_b
    norm_args = (q_norm_g, k_norm_g, mem_q_norm_g, pool_w_b, pool_scale)

    n = b * s
    x2d = x_prompt.reshape(n, D_MODEL)
    tm = _row_tile(s, 512)
    tables_p = _rope_tables(jnp.arange(s, dtype=jnp.int32))
    u_p, q_p, k_p, v_p, qm_p, yp_p, hb_p = _inproj_prompt(x2d, g1, w_in_b, tables_p, *norm_args,
                                                          batch=b, seq=s, tm=tm)
    mk2d, mv2d = _mem_kv(mem_prompt.reshape(b * m_tok, D_MODEL), w_mem_b, mem_k_norm_g, tm=m_tok)
    ya_p, w_up_b, w_down_b = _moba_prompt(q_p, k_p, v_p, cast=(w_up[0], w_down[0]))
    mk_p = mk2d.reshape(b, m_tok, MEM_HEADS, HEAD_DIM)
    mv_p = mv2d.reshape(b, m_tok, MEM_HEADS, HEAD_DIM)
    ym_p = _mem_attend(qm_p, mk2d.reshape(b, m_tok, D_MEM), mv2d.reshape(b, m_tok, D_MEM),
                       rows_per_seq=s, tq=_row_tile(s, 1024))
    x1_p = _merge(x2d, hb_p, yp_p, ya_p, ym_p, w_in_b, wbp_b, wba_b, wbm_b, wo_b, tm=tm)
    y_p, sta_p, stg_p, sums = _ffn_seq(x1_p, g2, w_up_b, conv_w[0], conv_b2, w_down_b, page_table, cache_k,
                                       tm=tm, batch=b, seq=s)

    xs2d = x_sample.reshape(db, D_MODEL)
    past_len = n_pages * PAGE_SIZE
    pos_s = jnp.full((db,), past_len, jnp.int32)
    u_s, q_s, k_s, v_s, qm_s, yp_s, hb_s = _inproj_sample(xs2d, g1, w_in_b, _rope_tables(pos_s), *norm_args,
                                                          state_pool[0], past_len=past_len)
    q3, k3, v3 = (a.reshape(db, 1, D_ATTN) for a in (q_s, k_s, v_s))
    sel = _block_gate(q3, sums)[:, :, :MOBA_TOPK].reshape(db, N_HEADS * MOBA_TOPK)
    ya_s = _moba_sample(page_table, sel, q3, k3, v3, cache_k, cache_v).reshape(db, D_ATTN)
    ym_s = _mem_attend_token(qm_s, cache_mem_k[0], cache_mem_v[0])
    x1_s = _merge(xs2d, hb_s, yp_s, ya_s, ym_s, w_in_b, wbp_b, wba_b, wbm_b, wo_b, tm=db)
    y_s, ua_s, ug_s = _ffn_step(x1_s, g2, w_up_b, conv_w[0], conv_b2, w_down_b,
                                state_conv[0, :, 0], state_conv[0, :, 1], tm=db)

    up_s = jnp.concatenate([ua_s, ug_s], axis=-1)
    return (
        y_p.reshape(b, s, D_MODEL),
        y_s.reshape(db, 1, D_MODEL),
        k_p[None], v_p[None],
        k_s.reshape(1, db, N_HEADS, 1, HEAD_DIM), v_s.reshape(1, db, N_HEADS, 1, HEAD_DIM),
        mk_p[None], mv_p[None],
        u_p.reshape(b, s, D_POOL)[None, :, s - POOL_STATE:],
        jnp.concatenate([state_pool[0][:, 1:], u_s[:, None]], axis=1)[None],
        jnp.concatenate([sta_p, stg_p], axis=-1)[None],
        jnp.concatenate([state_conv[0][:, 1:], up_s[:, None]], axis=1)[None],
    )
```

```python
import functools

import jax
import jax.numpy as jnp
import numpy as np
from jax import lax
from jax.experimental import pallas as pl
from jax.experimental.pallas import tpu as pltpu

F32 = jnp.float32
BF16 = jnp.bfloat16

D_MODEL = 2048
HEAD_DIM = 128
N_HEADS = 8
MEM_HEADS = 4
POOL_WINDOWS = (2, 4, 8, 16)
POOL_GROUP_DIM = 128
D_POOL = len(POOL_WINDOWS) * POOL_GROUP_DIM
POOL_STATE = max(POOL_WINDOWS) - 1
D_ATTN = N_HEADS * HEAD_DIM
D_MEM = MEM_HEADS * HEAD_DIM
D_QKV = D_POOL + 3 * D_ATTN + D_MEM
N_BRANCH = 3
MOBA_BLOCK = 256
MOBA_TOPK = 3
PAGE_SIZE = 128
PAGES_PER_BLOCK = MOBA_BLOCK // PAGE_SIZE
ROPE_THETA = 500000.0
ROPE_DIM = HEAD_DIM // 4
ROPE_HALF = ROPE_DIM // 2
D_FF = 5632
CONV_WIDTH = 3
EPS = 1e-6
NEG = -1e30
SCALE = HEAD_DIM ** -0.5
LOG2E = 1.4426950408889634

LANES = 128
SUBLANES = 8
BF16_ROWS = 2 * SUBLANES
POOL_HALO = 16
CONV_HALO = SUBLANES
VMEM_LIMIT = 56 * 1024 * 1024
FFN_CHUNK = 512
MERGE_CHUNK = 512
PAGE_DMA_PRIORITY = 1


def _params(*sem):
    return pltpu.CompilerParams(dimension_semantics=sem, vmem_limit_bytes=VMEM_LIMIT)


def _rms(x, g):
    ms = jnp.mean(x * x, axis=-1, keepdims=True)
    return x * lax.rsqrt(ms + EPS) * g


def _rope(t, cos, sin_lo, sin_hi):
    return (t * cos + pltpu.roll(t, ROPE_HALF, 1) * sin_hi
            + pltpu.roll(t, HEAD_DIM - ROPE_HALF, 1) * sin_lo)


def _rope_tables(pos):
    half = ROPE_HALF
    inv = ROPE_THETA ** (-jnp.arange(half, dtype=F32) * (2.0 / ROPE_DIM))
    ang = pos.astype(F32)[:, None] * inv[None, :]
    cos, sin = jnp.cos(ang), jnp.sin(ang)
    n = pos.shape[0]
    pad = jnp.zeros((n, HEAD_DIM - ROPE_DIM), F32)
    zero = jnp.zeros((n, half), F32)
    cos_t = jnp.concatenate([cos, cos, pad + 1.0], axis=1)
    sin_lo = jnp.concatenate([-sin, zero, pad], axis=1)
    sin_hi = jnp.concatenate([zero, sin, pad], axis=1)
    return cos_t, sin_lo, sin_hi


def _dot(a, b):
    return jnp.dot(a, b, preferred_element_type=F32)


def _dot_t(a, b):
    return lax.dot_general(a, b, (((1,), (1,)), ((), ())), preferred_element_type=F32)


def _top3(gs, n_valid):
    lane = lax.broadcasted_iota(jnp.int32, gs.shape, 1)
    out = []
    for r in range(MOBA_TOPK):
        mx = jnp.max(gs, axis=-1, keepdims=True)
        idx = jnp.min(jnp.where(gs == mx, lane, LANES), axis=-1, keepdims=True)
        out.append(jnp.where(r < n_valid, idx, -1))
        gs = jnp.where(lane == idx, -jnp.inf, gs)
    return out


def _pool_branch(u, win_sum, cnt, pw_ref, pscale_ref, yp_ref):
    for g, w in enumerate(POOL_WINDOWS):
        sl = slice(g * POOL_GROUP_DIM, (g + 1) * POOL_GROUP_DIM)
        p = win_sum(g, w) / cnt(w) - u[:, sl]
        y = _dot(p.astype(BF16), pw_ref[g]) * pscale_ref[:, sl]
        yp_ref[:, sl] = y.astype(yp_ref.dtype)


def _inproj_prompt_kernel(x_ref, g1_ref, w_ref, cos_ref, slo_ref, shi_ref, qg_ref, kg_ref, mqg_ref,
                          pw_ref, pscale_ref,
                          u_ref, q_ref, k_ref, v_ref, qm_ref, yp_ref, hb_ref,
                          uext_ref, *, tm, tiles_per_seq):
    i = pl.program_id(0)
    t_in_seq = i % tiles_per_seq
    hb = _rms(x_ref[...], g1_ref[...]).astype(BF16)
    hb_ref[...] = hb

    u = _dot(hb, w_ref[:, 0:D_POOL])
    u_ref[...] = u

    @pl.when(t_in_seq == 0)
    def _():
        uext_ref[0:POOL_HALO, :] = jnp.zeros((POOL_HALO, D_POOL), F32)

    uext_ref[POOL_HALO:POOL_HALO + tm, :] = u
    pos = t_in_seq * tm + lax.broadcasted_iota(jnp.int32, (tm, 1), 0)

    def win_sum(g, w):
        sl = slice(g * POOL_GROUP_DIM, (g + 1) * POOL_GROUP_DIM)
        s = uext_ref[POOL_HALO:POOL_HALO + tm, sl]
        for d in range(1, w):
            s = s + uext_ref[POOL_HALO - d:POOL_HALO - d + tm, sl]
        return s

    cos, slo, shi = cos_ref[...], slo_ref[...], shi_ref[...]
    c0 = D_POOL
    qf = _dot(hb, w_ref[:, c0:c0 + D_ATTN])
    for h in range(N_HEADS):
        t = _rms(qf[:, h * HEAD_DIM:(h + 1) * HEAD_DIM], qg_ref[...])
        q_ref[0, h] = _rope(t, cos, slo, shi)
    c0 += D_ATTN
    kf = _dot(hb, w_ref[:, c0:c0 + D_ATTN])
    for h in range(N_HEADS):
        t = _rms(kf[:, h * HEAD_DIM:(h + 1) * HEAD_DIM], kg_ref[...])
        k_ref[0, h] = _rope(t, cos, slo, shi)
    c0 += 2 * D_ATTN
    mf = _dot(hb, w_ref[:, c0:c0 + D_MEM])
    for h in range(MEM_HEADS):
        sl = slice(h * HEAD_DIM, (h + 1) * HEAD_DIM)
        qm_ref[:, sl] = _rms(mf[:, sl], mqg_ref[...]).astype(qm_ref.dtype)
    c0 -= D_ATTN
    vf = _dot(hb, w_ref[:, c0:c0 + D_ATTN])
    for h in range(N_HEADS):
        v_ref[0, h] = vf[:, h * HEAD_DIM:(h + 1) * HEAD_DIM]
    _pool_branch(u, win_sum, lambda w: jnp.minimum(w, pos + 1).astype(F32), pw_ref, pscale_ref, yp_ref)
    uext_ref[0:POOL_HALO, :] = uext_ref[tm:tm + POOL_HALO, :]


def _inproj_prompt(x2d, g1, w_in_b, tables, qg, kg, mqg, pool_w_b, pool_scale, *, batch, seq, tm):
    n = x2d.shape[0]
    tiles_per_seq = seq // tm
    row = lambda i: (i, 0)
    const = lambda i: (0, 0)
    tab = pl.BlockSpec((tm, HEAD_DIM), lambda i: (i % tiles_per_seq, 0))
    head_major = pl.BlockSpec((1, N_HEADS, tm, HEAD_DIM), lambda i: (i // tiles_per_seq, 0, i % tiles_per_seq, 0))
    hm_shape = jax.ShapeDtypeStruct((batch, N_HEADS, seq, HEAD_DIM), F32)
    return pl.pallas_call(
        functools.partial(_inproj_prompt_kernel, tm=tm, tiles_per_seq=tiles_per_seq),
        grid=(n // tm,),
        in_specs=[
            pl.BlockSpec((tm, D_MODEL), row),
            pl.BlockSpec((1, D_MODEL), const),
            pl.BlockSpec((D_MODEL, D_QKV), const, pipeline_mode=pl.Buffered(1)),
            tab, tab, tab,
            pl.BlockSpec((1, HEAD_DIM), const), pl.BlockSpec((1, HEAD_DIM), const), pl.BlockSpec((1, HEAD_DIM), const),
            pl.BlockSpec((len(POOL_WINDOWS), POOL_GROUP_DIM, POOL_GROUP_DIM), lambda i: (0, 0, 0)),
            pl.BlockSpec((1, D_POOL), const),
        ],
        out_specs=[
            pl.BlockSpec((tm, D_POOL), row), head_major, head_major, head_major,
            pl.BlockSpec((tm, D_MEM), row), pl.BlockSpec((tm, D_POOL), row), pl.BlockSpec((tm, D_MODEL), row),
        ],
        out_shape=[
            jax.ShapeDtypeStruct((n, D_POOL), F32), hm_shape, hm_shape, hm_shape,
            jax.ShapeDtypeStruct((n, D_MEM), BF16), jax.ShapeDtypeStruct((n, D_POOL), BF16),
            jax.ShapeDtypeStruct((n, D_MODEL), BF16),
        ],
        scratch_shapes=[pltpu.VMEM((tm + POOL_HALO, D_POOL), F32)],
        compiler_params=_params("arbitrary"),
        name="inproj_prompt",
    )(x2d, g1, w_in_b, *tables, qg, kg, mqg, pool_w_b, pool_scale)


def _inproj_sample_kernel(x_ref, g1_ref, w_ref, cos_ref, slo_ref, shi_ref, qg_ref, kg_ref, mqg_ref,
                          pw_ref, pscale_ref, state_ref,
                          u_ref, q_ref, k_ref, v_ref, qm_ref, yp_ref, hb_ref, *, past_len):
    hb = _rms(x_ref[...], g1_ref[...]).astype(BF16)
    hb_ref[...] = hb
    u = _dot(hb, w_ref[:, 0:D_POOL])
    u_ref[...] = u
    db = state_ref.shape[0]
    srow = lax.broadcasted_iota(jnp.int32, (db, POOL_STATE, POOL_GROUP_DIM), 1)

    def win_sum(g, w):
        sl = slice(g * POOL_GROUP_DIM, (g + 1) * POOL_GROUP_DIM)
        hist = jnp.where(srow >= POOL_STATE - (w - 1), state_ref[:, :, sl], 0.0)
        return u[:, sl] + jnp.sum(hist, axis=1)

    _pool_branch(u, win_sum, lambda w: float(min(w, past_len + 1)), pw_ref, pscale_ref, yp_ref)

    cos, slo, shi = cos_ref[...], slo_ref[...], shi_ref[...]
    c0 = D_POOL
    qf = _dot(hb, w_ref[:, c0:c0 + D_ATTN])
    c0 += D_ATTN
    kf = _dot(hb, w_ref[:, c0:c0 + D_ATTN])
    for h in range(N_HEADS):
        sl = slice(h * HEAD_DIM, (h + 1) * HEAD_DIM)
        q_ref[:, sl] = _rope(_rms(qf[:, sl], qg_ref[...]), cos, slo, shi)
        k_ref[:, sl] = _rope(_rms(kf[:, sl], kg_ref[...]), cos, slo, shi)
    c0 += D_ATTN
    v_ref[...] = _dot(hb, w_ref[:, c0:c0 + D_ATTN])
    c0 += D_ATTN
    mf = _dot(hb, w_ref[:, c0:c0 + D_MEM])
    for h in range(MEM_HEADS):
        sl = slice(h * HEAD_DIM, (h + 1) * HEAD_DIM)
        qm_ref[:, sl] = _rms(mf[:, sl], mqg_ref[...]).astype(qm_ref.dtype)


def _inproj_sample(x2d, g1, w_in_b, tables, qg, kg, mqg, pool_w_b, pool_scale, state_pool, *, past_len):
    db = x2d.shape[0]
    full = lambda shape: pl.BlockSpec(shape, lambda i: (0,) * len(shape))
    return pl.pallas_call(
        functools.partial(_inproj_sample_kernel, past_len=past_len),
        grid=(1,),
        in_specs=[
            full((db, D_MODEL)), full((1, D_MODEL)),
            pl.BlockSpec((D_MODEL, D_QKV), lambda i: (0, 0), pipeline_mode=pl.Buffered(1)),
            full((db, HEAD_DIM)), full((db, HEAD_DIM)), full((db, HEAD_DIM)),
            full((1, HEAD_DIM)), full((1, HEAD_DIM)), full((1, HEAD_DIM)),
            full((len(POOL_WINDOWS), POOL_GROUP_DIM, POOL_GROUP_DIM)), full((1, D_POOL)),
            full((db, POOL_STATE, D_POOL)),
        ],
        out_specs=[full((db, D_POOL)), full((db, D_ATTN)), full((db, D_ATTN)), full((db, D_ATTN)),
                   full((db, D_MEM)), full((db, D_POOL)), full((db, D_MODEL))],
        out_shape=[jax.ShapeDtypeStruct((db, D_POOL), F32), jax.ShapeDtypeStruct((db, D_ATTN), F32),
                   jax.ShapeDtypeStruct((db, D_ATTN), F32), jax.ShapeDtypeStruct((db, D_ATTN), F32),
                   jax.ShapeDtypeStruct((db, D_MEM), F32), jax.ShapeDtypeStruct((db, D_POOL), BF16),
                   jax.ShapeDtypeStruct((db, D_MODEL), BF16)],
        compiler_params=_params("arbitrary"),
        name="inproj_sample",
    )(x2d, g1, w_in_b, *tables, qg, kg, mqg, pool_w_b, pool_scale, state_pool)


def _moba_prompt_kernel(q_ref, k_ref, v_ref, *rest, seq, n_cast):
    o_ref = rest[n_cast]
    qa_ref, ka_ref, vb_ref, km_ref, gs_ref, bt_ref, tri_ref, s_ref, p_ref = rest[2 * n_cast + 1:]
    for src_ref, dst_ref in zip(rest[:n_cast], rest[n_cast + 1:2 * n_cast + 1]):
        dst_ref[...] = src_ref[...].astype(BF16)
    nb = seq // MOBA_BLOCK
    nbp = -(-nb // SUBLANES) * SUBLANES
    shift = MOBA_BLOCK.bit_length() - 1
    q = q_ref[0, 0]
    k = k_ref[0, 0]
    row_blk = lax.shift_right_logical(lax.broadcasted_iota(jnp.int32, (seq, LANES), 0), shift)
    lane = lax.broadcasted_iota(jnp.int32, (seq, LANES), 1)
    ka_ref[:, 0:HEAD_DIM] = k.astype(BF16)
    ka_ref[:, HEAD_DIM:] = jnp.where(lane == row_blk, 1.0, 0.0).astype(BF16)
    vb_ref[:, 0:HEAD_DIM] = v_ref[0, 0].astype(BF16)
    vb_ref[:, HEAD_DIM:] = jnp.ones((seq, HEAD_DIM), BF16)
    qa_ref[:, 0:HEAD_DIM] = (q * (SCALE * LOG2E)).astype(BF16)

    km_ref[...] = jnp.zeros(km_ref.shape, F32)
    km_ref[0:nb, :] = jnp.sum(k.reshape(nb, MOBA_BLOCK, HEAD_DIM), axis=1) * (1.0 / MOBA_BLOCK)
    gst = _dot_t(km_ref[...].astype(BF16), q.astype(BF16))
    jidx = lax.broadcasted_iota(jnp.int32, (nbp, seq), 0)
    qblk = lax.shift_right_logical(lax.broadcasted_iota(jnp.int32, (nbp, seq), 1), shift)
    past = jidx < qblk
    gs = jnp.where(past, gst[0:nbp, :], NEG)
    gs_ref[...] = gs
    rank = jnp.zeros((nbp, seq), F32)
    for jp in range(nb):
        other = gs_ref[jp:jp + 1, :]
        rank = rank + jnp.where(jidx > jp, jnp.where(other >= gs, 1.0, 0.0), jnp.where(other > gs, 1.0, 0.0))
    bt_ref[...] = jnp.zeros(bt_ref.shape, F32)
    bt_ref[0:nbp, :] = jnp.where(past, jnp.where(rank < MOBA_TOPK, 0.0, NEG), jnp.where(jidx == qblk, 0.0, NEG))
    for c in range(seq // MOBA_BLOCK):
        rows = slice(c * MOBA_BLOCK, (c + 1) * MOBA_BLOCK)
        qa_ref[rows, HEAD_DIM:] = bt_ref[:, rows].T.astype(BF16)

    tri_ref[...] = jnp.where(lax.broadcasted_iota(jnp.int32, tri_ref.shape, 1)
                             <= lax.broadcasted_iota(jnp.int32, tri_ref.shape, 0), 0.0, NEG)

    for bq in range(nb):
        par = bq % 2
        n_keys = bq + 1
        rows = slice(bq * MOBA_BLOCK, (bq + 1) * MOBA_BLOCK)
        qa = qa_ref[rows, :]
        m_part = None
        for j in range(n_keys):
            s = _dot_t(qa, ka_ref[j * MOBA_BLOCK:(j + 1) * MOBA_BLOCK, :])
            if j == bq:
                s = s + tri_ref[...]
            s_ref[par, j] = s
            mj = jnp.maximum(s[:, :LANES], s[:, LANES:])
            m_part = mj if m_part is None else jnp.maximum(m_part, mj)
        m = jnp.broadcast_to(jnp.max(m_part, axis=-1, keepdims=True), (MOBA_BLOCK, LANES))
        for j in range(n_keys):
            for half in range(MOBA_BLOCK // LANES):
                cols = slice(half * LANES, (half + 1) * LANES)
                p = jnp.exp2(s_ref[par, j, :, cols] - m)
                p_ref[par, :, j * MOBA_BLOCK + half * LANES:j * MOBA_BLOCK + (half + 1) * LANES] = p.astype(BF16)
        ol = _dot(p_ref[par, :, 0:n_keys * MOBA_BLOCK], vb_ref[0:n_keys * MOBA_BLOCK, :])
        o_ref[rows, :] = (ol[:, 0:HEAD_DIM] / ol[:, HEAD_DIM:]).astype(o_ref.dtype)


def _slab_rows(rows, n_steps):
    need = -(-rows // n_steps)
    for r in range(BF16_ROWS, rows + 1, BF16_ROWS):
        if r >= need and rows % r == 0:
            return r
    return rows


def _moba_prompt(q, k, v, cast=()):
    b, h, s, _ = q.shape
    nb = s // MOBA_BLOCK
    assert nb <= LANES
    nbp = -(-nb // SUBLANES) * SUBLANES
    blk = pl.BlockSpec((1, 1, s, HEAD_DIM), lambda bi, hi: (bi, hi, 0, 0))

    def slab_spec(w):
        r = _slab_rows(w.shape[0], b * h)
        last = w.shape[0] // r - 1
        return pl.BlockSpec((r, w.shape[1]), lambda bi, hi: (jnp.minimum(bi * h + hi, last), 0))

    slabs = [slab_spec(w) for w in cast]
    out = pl.pallas_call(
        functools.partial(_moba_prompt_kernel, seq=s, n_cast=len(cast)),
        grid=(b, h),
        in_specs=[blk, blk, blk] + slabs,
        out_specs=[pl.BlockSpec((s, HEAD_DIM), lambda bi, hi: (bi, hi))] + slabs,
        out_shape=[jax.ShapeDtypeStruct((b * s, h * HEAD_DIM), BF16)]
        + [jax.ShapeDtypeStruct(w.shape, BF16) for w in cast],
        scratch_shapes=[pltpu.VMEM((s, 2 * HEAD_DIM), BF16), pltpu.VMEM((s, 2 * HEAD_DIM), BF16),
                        pltpu.VMEM((s, 2 * HEAD_DIM), BF16),
                        pltpu.VMEM((LANES, HEAD_DIM), F32), pltpu.VMEM((nbp, s), F32), pltpu.VMEM((LANES, s), F32),
                        pltpu.VMEM((MOBA_BLOCK, MOBA_BLOCK), F32),
                        pltpu.VMEM((2, nb, MOBA_BLOCK, MOBA_BLOCK), F32), pltpu.VMEM((2, MOBA_BLOCK, s), BF16)],
        compiler_params=_params("arbitrary", "arbitrary"),
        name="moba_prompt",
    )(q, k, v, *cast)
    return out if cast else out[0]


def _mem_attend_kernel(q_ref, k_ref, v_ref, o_ref):
    for h in range(MEM_HEADS):
        sl = slice(h * HEAD_DIM, (h + 1) * HEAD_DIM)
        qs = (q_ref[:, sl].astype(F32) * SCALE).astype(BF16)
        s = _dot_t(qs, k_ref[0, :, sl].astype(BF16))
        m = jnp.max(s, axis=-1, keepdims=True)
        p = jnp.exp(s - m)
        l = jnp.sum(p, axis=-1, keepdims=True)
        o = _dot(p.astype(BF16), v_ref[0, :, sl].astype(BF16)) / l
        o_ref[:, sl] = o.astype(o_ref.dtype)


def _mem_attend(qm, mem_k, mem_v, *, rows_per_seq, tq):
    n = qm.shape[0]
    m_tok = mem_k.shape[1]
    tiles = rows_per_seq // tq
    kv = pl.BlockSpec((1, m_tok, D_MEM), lambda i: (i // tiles, 0, 0))
    return pl.pallas_call(
        _mem_attend_kernel,
        grid=(n // tq,),
        in_specs=[pl.BlockSpec((tq, D_MEM), lambda i: (i, 0)), kv, kv],
        out_specs=pl.BlockSpec((tq, D_MEM), lambda i: (i, 0)),
        out_shape=jax.ShapeDtypeStruct((n, D_MEM), BF16),
        compiler_params=_params("arbitrary"),
        name="mem_attend",
    )(qm, mem_k, mem_v)


def _mem_attend_token_kernel(q_ref, k_ref, v_ref, o_ref, *, seqs):
    for sq in range(seqs):
        for h in range(MEM_HEADS):
            sl = slice(h * HEAD_DIM, (h + 1) * HEAD_DIM)
            qh = q_ref[sq:sq + 1, sl] * SCALE
            s = jnp.sum(k_ref[sq, :, h, :] * qh, axis=-1, keepdims=True)
            p = jnp.exp(s - jnp.max(s, axis=0, keepdims=True))
            o = jnp.sum(p * v_ref[sq, :, h, :], axis=0, keepdims=True) / jnp.sum(p, axis=0, keepdims=True)
            o_ref[sq:sq + 1, sl] = o


def _mem_attend_token(qm, mem_k, mem_v):
    db = qm.shape[0]
    seqs = _row_tile(db, SUBLANES)
    kv = pl.BlockSpec((seqs,) + mem_k.shape[1:], lambda i: (i, 0, 0, 0))
    return pl.pallas_call(
        functools.partial(_mem_attend_token_kernel, seqs=seqs),
        grid=(db // seqs,),
        in_specs=[pl.BlockSpec((seqs, D_MEM), lambda i: (i, 0)), kv, kv],
        out_specs=pl.BlockSpec((seqs, D_MEM), lambda i: (i, 0)),
        out_shape=jax.ShapeDtypeStruct((db, D_MEM), F32),
        compiler_params=_params("arbitrary"),
        name="mem_attend_token",
    )(qm, mem_k, mem_v)


def _mem_kv_kernel(x_ref, w_ref, g_ref, k_ref, v_ref):
    kv = _dot(x_ref[...].astype(BF16), w_ref[...])
    for h in range(MEM_HEADS):
        sl = slice(h * HEAD_DIM, (h + 1) * HEAD_DIM)
        k_ref[:, sl] = _rms(kv[:, sl], g_ref[...])
    v_ref[...] = kv[:, D_MEM:]


def _mem_kv(mem2d, w_b, g, *, tm):
    n = mem2d.shape[0]
    out = jax.ShapeDtypeStruct((n, D_MEM), F32)
    return pl.pallas_call(
        _mem_kv_kernel,
        grid=(n // tm,),
        in_specs=[pl.BlockSpec((tm, D_MODEL), lambda i: (i, 0)),
                  pl.BlockSpec((D_MODEL, 2 * D_MEM), lambda i: (0, 0)),
                  pl.BlockSpec((1, HEAD_DIM), lambda i: (0, 0))],
        out_specs=[pl.BlockSpec((tm, D_MEM), lambda i: (i, 0))] * 2,
        out_shape=[out, out],
        compiler_params=_params("arbitrary"),
        name="mem_kv",
    )(mem2d, w_b, g)


def _merge_kernel(x_ref, hb_ref, yp_ref, ya_ref, ym_ref, wg0_ref, wg1_ref, wg2_ref,
                  wbp_ref, wba_ref, wbm_ref, wo_ref, o_ref):
    @pl.when(pl.program_id(1) == 0)
    def _():
        o_ref[...] = x_ref[...]

    hb = hb_ref[...]
    merged = (jax.nn.sigmoid(_dot(hb, wg0_ref[...])) * _dot(yp_ref[...].astype(BF16), wbp_ref[...])
              + jax.nn.sigmoid(_dot(hb, wg1_ref[...])) * _dot(ya_ref[...].astype(BF16), wba_ref[...])
              + jax.nn.sigmoid(_dot(hb, wg2_ref[...])) * _dot(ym_ref[...].astype(BF16), wbm_ref[...]))
    o_ref[...] += _dot(merged.astype(BF16), wo_ref[...])


def _merge(x2d, hb, yp, ya, ym, w_in_b, wbp_b, wba_b, wbm_b, wo_b, *, tm):
    n = x2d.shape[0]
    tc = MERGE_CHUNK
    n_chunks = D_MODEL // tc
    gate0 = D_QKV // tc
    row = lambda i, c: (i, 0)

    def gate_spec(branch):
        return pl.BlockSpec((D_MODEL, tc), lambda i, c: (0, gate0 + branch * n_chunks + c))

    col = lambda rows: pl.BlockSpec((rows, tc), lambda i, c: (0, c))
    return pl.pallas_call(
        _merge_kernel,
        grid=(n // tm, n_chunks),
        in_specs=[pl.BlockSpec((tm, D_MODEL), row), pl.BlockSpec((tm, D_MODEL), row),
                  pl.BlockSpec((tm, D_POOL), row), pl.BlockSpec((tm, D_ATTN), row), pl.BlockSpec((tm, D_MEM), row),
                  gate_spec(0), gate_spec(1), gate_spec(2),
                  col(D_POOL), col(D_ATTN), col(D_MEM),
                  pl.BlockSpec((tc, D_MODEL), lambda i, c: (c, 0))],
        out_specs=pl.BlockSpec((tm, D_MODEL), row),
        out_shape=jax.ShapeDtypeStruct((n, D_MODEL), F32),
        compiler_params=_params("arbitrary", "arbitrary"),
        name="merge",
    )(x2d, hb, yp, ya, ym, w_in_b, w_in_b, w_in_b, wbp_b, wba_b, wbm_b, wo_b)


def _ffn_act(ua, ug, a_m1, a_m2, g_m1, g_m2, cwa_ref, cwg_ref, cba_ref, cbg_ref):
    ca = cba_ref[...] + cwa_ref[0:1, :] * a_m2 + cwa_ref[1:2, :] * a_m1 + cwa_ref[2:3, :] * ua
    cg = cbg_ref[...] + cwg_ref[0:1, :] * g_m2 + cwg_ref[1:2, :] * g_m1 + cwg_ref[2:3, :] * ug
    return (jax.nn.silu(ca) * cg).astype(BF16)


def _side_page_copy(pt_ref, cache_ref, buf_ref, sem_ref, group, k, *, total_pages, pages_per_step):
    idx = jnp.minimum(group * pages_per_step + k, total_pages - 1)
    page = pt_ref[idx]
    half = lax.rem(group, 2)
    return pltpu.make_async_copy(cache_ref.at[0, page], buf_ref.at[half, k], sem_ref.at[half, k])


def _ffn_seq_kernel(pt_ref, xu_ref, g2_ref, wua_ref, wug_ref, cwa_ref, cwg_ref, cba_ref, cbg_ref, wd_ref, cache_ref,
                    o_ref, sta_ref, stg_ref, psum_ref,
                    hb_ref, exta_ref, extg_ref, carry_ref, pbuf_ref, psem_ref,
                    *, tm, tiles_per_seq, n_chunks, n_steps, total_pages, pages_per_step, n_groups):
    t = pl.program_id(0)
    tu = jnp.minimum(t, n_steps - 1)
    i, c = tu // n_chunks, tu % n_chunks
    n_prev = CONV_WIDTH - 1
    lo = CONV_HALO - n_prev
    page_copy = functools.partial(_side_page_copy, pt_ref, cache_ref, pbuf_ref, psem_ref,
                                  total_pages=total_pages, pages_per_step=pages_per_step)

    @pl.when(t == 0)
    def _():
        exta_ref[1] = jnp.zeros(exta_ref.shape[1:], F32)
        extg_ref[1] = jnp.zeros(extg_ref.shape[1:], F32)
        pbuf_ref[...] = jnp.zeros(pbuf_ref.shape, F32)
        for k in range(pages_per_step):
            page_copy(0, k).start(priority=PAGE_DMA_PRIORITY)

    @pl.when(t + 1 < n_groups)
    def _():
        for k in range(pages_per_step):
            page_copy(t + 1, k).start(priority=PAGE_DMA_PRIORITY)

    @pl.when(t < n_groups)
    def _():
        for k in range(pages_per_step):
            page_copy(t, k).wait()

    @pl.when(c == 0)
    def _():
        hb_ref[...] = _rms(xu_ref[...], g2_ref[...]).astype(BF16)

    @pl.when((t == 0) | (t % n_chunks == 1))
    def _():
        o_ref[...] = xu_ref[...]

    @pl.when(i % tiles_per_seq == 0)
    def _():
        carry_ref[c] = jnp.zeros(carry_ref.shape[1:], F32)

    cur, prv = t % 2, (t + 1) % 2
    hb = hb_ref[...]
    ua = _dot(hb, wua_ref[...])
    ug = _dot(hb, wug_ref[...])
    exta_ref[cur, lo:CONV_HALO, :] = carry_ref[c, 0:n_prev, :]
    extg_ref[cur, lo:CONV_HALO, :] = carry_ref[c, n_prev:2 * n_prev, :]
    exta_ref[cur, CONV_HALO:CONV_HALO + tm, :] = ua
    extg_ref[cur, CONV_HALO:CONV_HALO + tm, :] = ug
    a_last = ua[tm - n_prev:tm, :]
    g_last = ug[tm - n_prev:tm, :]
    carry_ref[c, 0:n_prev, :] = a_last
    carry_ref[c, n_prev:2 * n_prev, :] = g_last

    rows = lambda ref, back: ref[prv, CONV_HALO - back:CONV_HALO - back + tm, :]
    act = _ffn_act(rows(exta_ref, 0), rows(extg_ref, 0), rows(exta_ref, 1), rows(exta_ref, 2),
                   rows(extg_ref, 1), rows(extg_ref, 2), cwa_ref, cwg_ref, cba_ref, cbg_ref)
    o_ref[...] += _dot(act, wd_ref[...])

    for k in range(pages_per_step):
        psum_ref[k] = jnp.sum(pbuf_ref[cur, k], axis=1)

    @pl.when(i % tiles_per_seq == tiles_per_seq - 1)
    def _():
        sta_ref[i // tiles_per_seq, c] = a_last
        stg_ref[i // tiles_per_seq, c] = g_last


def _ffn_step_kernel(x_ref, g2_ref, wua_ref, wug_ref, cwa_ref, cwg_ref, cba_ref, cbg_ref, wd_ref,
                     am2_ref, gm2_ref, am1_ref, gm1_ref,
                     o_ref, ua_ref, ug_ref, hb_ref, acc_ref):
    c = pl.program_id(1)

    @pl.when(c == 0)
    def _():
        hb_ref[...] = _rms(x_ref[...], g2_ref[...]).astype(BF16)
        acc_ref[...] = jnp.zeros(acc_ref.shape, F32)

    hb = hb_ref[...]
    ua = _dot(hb, wua_ref[...])
    ug = _dot(hb, wug_ref[...])
    ua_ref[...] = ua
    ug_ref[...] = ug
    act = _ffn_act(ua, ug, am1_ref[...], am2_ref[...], gm1_ref[...], gm2_ref[...], cwa_ref, cwg_ref, cba_ref, cbg_ref)
    acc_ref[...] += _dot(act, wd_ref[...])

    @pl.when(c == pl.num_programs(1) - 1)
    def _():
        o_ref[...] = x_ref[...] + acc_ref[...]


def _ffn_seq(x2d, g2, w_up_b, conv_w, conv_b2, w_down_b, page_table, cache_k, *, tm, batch, seq):
    n = x2d.shape[0]
    tc = FFN_CHUNK
    n_chunks = D_FF // tc
    assert n_chunks >= 2
    tiles_per_seq = seq // tm
    n_prev = CONV_WIDTH - 1
    n_steps = (n // tm) * n_chunks
    db, n_pages = page_table.shape
    total_pages = db * n_pages
    pages_per_step = -(-total_pages // n_steps)
    n_groups = -(-total_pages // pages_per_step)
    up_tile = lambda t: jnp.minimum(t, n_steps - 1) // n_chunks
    up_chunk = lambda t: jnp.minimum(t, n_steps - 1) % n_chunks
    down_tile = lambda t: jnp.maximum(t - 1, 0) // n_chunks
    down_chunk = lambda t: jnp.maximum(t - 1, 0) % n_chunks
    a_up = lambda rows: pl.BlockSpec((rows, tc), lambda t, pt: (0, up_chunk(t)))
    g_up = lambda rows: pl.BlockSpec((rows, tc), lambda t, pt: (0, n_chunks + up_chunk(t)))
    a_dn = lambda rows: pl.BlockSpec((rows, tc), lambda t, pt: (0, down_chunk(t)))
    g_dn = lambda rows: pl.BlockSpec((rows, tc), lambda t, pt: (0, n_chunks + down_chunk(t)))
    st_spec = pl.BlockSpec((batch, n_chunks, n_prev, tc), lambda t, pt: (0, 0, 0, 0))
    st_shape = jax.ShapeDtypeStruct((batch, n_chunks, n_prev, tc), F32)
    ps_spec = pl.BlockSpec((pages_per_step, N_HEADS, HEAD_DIM), lambda t, pt: (jnp.minimum(t, n_groups), 0, 0))
    ps_shape = jax.ShapeDtypeStruct(((n_groups + 1) * pages_per_step, N_HEADS, HEAD_DIM), F32)
    y, sta, stg, psums = pl.pallas_call(
        functools.partial(_ffn_seq_kernel, tm=tm, tiles_per_seq=tiles_per_seq, n_chunks=n_chunks, n_steps=n_steps,
                          total_pages=total_pages, pages_per_step=pages_per_step, n_groups=n_groups),
        grid_spec=pltpu.PrefetchScalarGridSpec(
            num_scalar_prefetch=1,
            grid=(n_steps + 1,),
            in_specs=[pl.BlockSpec((tm, D_MODEL), lambda t, pt: (up_tile(t), 0)),
                      pl.BlockSpec((1, D_MODEL), lambda t, pt: (0, 0)),
                      a_up(D_MODEL), g_up(D_MODEL), a_dn(CONV_WIDTH), g_dn(CONV_WIDTH), a_dn(1), g_dn(1),
                      pl.BlockSpec((tc, D_MODEL), lambda t, pt: (down_chunk(t), 0)),
                      pl.BlockSpec(memory_space=pl.ANY)],
            out_specs=[pl.BlockSpec((tm, D_MODEL), lambda t, pt: (down_tile(t), 0)), st_spec, st_spec, ps_spec],
            scratch_shapes=[pltpu.VMEM((tm, D_MODEL), BF16),
                            pltpu.VMEM((2, tm + CONV_HALO, tc), F32), pltpu.VMEM((2, tm + CONV_HALO, tc), F32),
                            pltpu.VMEM((n_chunks, SUBLANES, tc), F32),
                            pltpu.VMEM((2, pages_per_step, N_HEADS, PAGE_SIZE, HEAD_DIM), F32),
                            pltpu.SemaphoreType.DMA((2, pages_per_step))],
        ),
        out_shape=[jax.ShapeDtypeStruct((n, D_MODEL), F32), st_shape, st_shape, ps_shape],
        compiler_params=_params("arbitrary"),
        name="ffn_seq",
    )(page_table.reshape(total_pages), x2d, g2, w_up_b, w_up_b, conv_w, conv_w, conv_b2, conv_b2, w_down_b, cache_k)
    unchunk = lambda st: st.transpose(0, 2, 1, 3).reshape(batch, n_prev, D_FF)
    return y, unchunk(sta), unchunk(stg), psums


def _ffn_step(x2d, g2, w_up_b, conv_w, conv_b2, w_down_b, prev2, prev1, *, tm):
    n = x2d.shape[0]
    tc = FFN_CHUNK
    n_chunks = D_FF // tc
    row = lambda i, c: (i, 0)
    a_col = lambda rows: pl.BlockSpec((rows, tc), lambda i, c: (0, c))
    g_col = lambda rows: pl.BlockSpec((rows, tc), lambda i, c: (0, n_chunks + c))
    rows_a = pl.BlockSpec((tm, tc), lambda i, c: (i, c))
    rows_g = pl.BlockSpec((tm, tc), lambda i, c: (i, n_chunks + c))
    up_shape = jax.ShapeDtypeStruct((n, D_FF), F32)
    return pl.pallas_call(
        _ffn_step_kernel,
        grid=(n // tm, n_chunks),
        in_specs=[pl.BlockSpec((tm, D_MODEL), row), pl.BlockSpec((1, D_MODEL), lambda i, c: (0, 0)),
                  a_col(D_MODEL), g_col(D_MODEL), a_col(CONV_WIDTH), g_col(CONV_WIDTH), a_col(1), g_col(1),
                  pl.BlockSpec((tc, D_MODEL), lambda i, c: (c, 0)),
                  rows_a, rows_g, rows_a, rows_g],
        out_specs=[pl.BlockSpec((tm, D_MODEL), row), rows_a, rows_a],
        out_shape=[jax.ShapeDtypeStruct((n, D_MODEL), F32), up_shape, up_shape],
        scratch_shapes=[pltpu.VMEM((tm, D_MODEL), BF16), pltpu.VMEM((tm, D_MODEL), F32)],
        compiler_params=_params("arbitrary", "arbitrary"),
        name="ffn_step",
    )(x2d, g2, w_up_b, w_up_b, conv_w, conv_w, conv_b2, conv_b2, w_down_b, prev2, prev2, prev1, prev1)


def _block_gate_kernel(q_ref, ps_ref, o_ref, km_ref, *, n_blocks, seqs):
    n_pages = n_blocks * PAGES_PER_BLOCK
    km_ref[...] = jnp.zeros(km_ref.shape, F32)
    lane = lax.broadcasted_iota(jnp.int32, (N_HEADS, LANES), 1)
    row = lax.broadcasted_iota(jnp.int32, (N_HEADS, LANES), 0)

    def one_seq(sq, carry):
        gs_all = jnp.full((N_HEADS, LANES), NEG, F32)
        base = pl.multiple_of(sq * n_pages, PAGES_PER_BLOCK)
        for h in range(N_HEADS):
            even = ps_ref[pl.ds(base, n_blocks, stride=PAGES_PER_BLOCK), h, :]
            odd = ps_ref[pl.ds(base + 1, n_blocks, stride=PAGES_PER_BLOCK), h, :]
            km_ref[h, 0:n_blocks, :] = (even + odd) * (1.0 / MOBA_BLOCK)
            qh = jnp.broadcast_to(q_ref[sq, :, h * HEAD_DIM:(h + 1) * HEAD_DIM], (N_HEADS, HEAD_DIM))
            gs = _dot_t(qh.astype(BF16), km_ref[h].astype(BF16))
            gs_all = jnp.where(row == h, gs, gs_all)
        gs_all = jnp.where(lane < n_blocks, gs_all, NEG)
        i0, i1, i2 = _top3(gs_all, n_blocks)
        o_ref[sq] = jnp.where(lane == 0, i0, jnp.where(lane == 1, i1, jnp.where(lane == 2, i2, 0)))
        return carry

    lax.fori_loop(0, seqs, one_seq, 0)


def _block_gate(q3, page_sums, n_pages):
    db = q3.shape[0]
    n_blocks = n_pages // PAGES_PER_BLOCK
    assert PAGES_PER_BLOCK == 2 and MOBA_TOPK <= n_blocks <= LANES and N_HEADS == SUBLANES
    assert page_sums.shape[0] >= db * n_pages
    seqs = _row_tile(db, SUBLANES)
    return pl.pallas_call(
        functools.partial(_block_gate_kernel, n_blocks=n_blocks, seqs=seqs),
        grid=(db // seqs,),
        in_specs=[pl.BlockSpec((seqs, 1, D_ATTN), lambda b: (b, 0, 0)),
                  pl.BlockSpec((seqs * n_pages, N_HEADS, HEAD_DIM), lambda b: (b, 0, 0))],
        out_specs=pl.BlockSpec((seqs, N_HEADS, LANES), lambda b: (b, 0, 0)),
        out_shape=jax.ShapeDtypeStruct((db, N_HEADS, LANES), jnp.int32),
        scratch_shapes=[pltpu.VMEM((N_HEADS, LANES, HEAD_DIM), F32)],
        compiler_params=_params("arbitrary"),
        name="block_gate",
    )(q3, page_sums)


N_SEL_PAGES = MOBA_TOPK * PAGES_PER_BLOCK


def _sel_copy(pt_ref, sel_ref, cache_ref, buf_ref, sem_ref, b, h, s):
    blk = sel_ref[b, h * MOBA_TOPK + s // PAGES_PER_BLOCK]
    page = pt_ref[b, blk * PAGES_PER_BLOCK + s % PAGES_PER_BLOCK]
    half = lax.rem(b, 2)
    return pltpu.make_async_copy(cache_ref.at[0, page, h], buf_ref.at[half, h, s], sem_ref.at[half, h])


def _moba_sample_kernel(pt_ref, sel_ref, q_ref, kn_ref, vn_ref, ck_ref, cv_ref, o_ref,
                        kbuf_ref, vbuf_ref, ksem_ref, vsem_ref):
    b = pl.program_id(0)

    def fetch(seq):
        for h in range(N_HEADS):
            for s in range(N_SEL_PAGES):
                _sel_copy(pt_ref, sel_ref, ck_ref, kbuf_ref, ksem_ref, seq, h, s).start(priority=0)
                _sel_copy(pt_ref, sel_ref, cv_ref, vbuf_ref, vsem_ref, seq, h, s).start(priority=1)

    @pl.when(b == 0)
    def _():
        fetch(b)

    @pl.when(b + 1 < pl.num_programs(0))
    def _():
        fetch(b + 1)

    half = lax.rem(b, 2)
    for h in range(N_HEADS):
        for s in range(N_SEL_PAGES):
            _sel_copy(pt_ref, sel_ref, ck_ref, kbuf_ref, ksem_ref, b, h, s).wait()
            _sel_copy(pt_ref, sel_ref, cv_ref, vbuf_ref, vsem_ref, b, h, s).wait()
        sl = slice(h * HEAD_DIM, (h + 1) * HEAD_DIM)
        qs = q_ref[0, :, sl] * SCALE
        kk = kbuf_ref[half, h].reshape(N_SEL_PAGES * PAGE_SIZE, HEAD_DIM)
        vv = vbuf_ref[half, h].reshape(N_SEL_PAGES * PAGE_SIZE, HEAD_DIM)
        s_sel = jnp.sum(kk * qs, axis=-1, keepdims=True)
        s_own = jnp.sum(qs * kn_ref[0, :, sl], axis=-1, keepdims=True)
        m = jnp.maximum(jnp.max(s_sel, axis=0, keepdims=True), s_own)
        p = jnp.exp(s_sel - m)
        p_own = jnp.exp(s_own - m)
        l = jnp.sum(p, axis=0, keepdims=True) + p_own
        o = (jnp.sum(p * vv, axis=0, keepdims=True) + p_own * vn_ref[0, :, sl]) / l
        o_ref[0, :, sl] = o.astype(o_ref.dtype)


def _moba_sample(page_table, sel, q3, k3, v3, cache_k, cache_v):
    db = q3.shape[0]
    tok = pl.BlockSpec((1, 1, D_ATTN), lambda b, pt, sl: (b, 0, 0))
    return pl.pallas_call(
        _moba_sample_kernel,
        grid_spec=pltpu.PrefetchScalarGridSpec(
            num_scalar_prefetch=2,
            grid=(db,),
            in_specs=[tok, tok, tok, pl.BlockSpec(memory_space=pl.ANY), pl.BlockSpec(memory_space=pl.ANY)],
            out_specs=tok,
            scratch_shapes=[pltpu.VMEM((2, N_HEADS, N_SEL_PAGES, PAGE_SIZE, HEAD_DIM), F32),
                            pltpu.VMEM((2, N_HEADS, N_SEL_PAGES, PAGE_SIZE, HEAD_DIM), F32),
                            pltpu.SemaphoreType.DMA((2, N_HEADS)), pltpu.SemaphoreType.DMA((2, N_HEADS))],
        ),
        out_shape=jax.ShapeDtypeStruct((db, 1, D_ATTN), F32),
        compiler_params=_params("arbitrary"),
        name="moba_sample",
    )(page_table, sel, q3, k3, v3, cache_k, cache_v)


def _row_tile(n, pref):
    return pref if n % pref == 0 else n


def kernel(x_prompt, x_sample, mem_prompt, cache_k, cache_v, page_table, cache_mem_k, cache_mem_v, state_pool, state_conv, norm1_g, w_in, q_norm_g, k_norm_g, mem_q_norm_g, mem_k_norm_g, w_mem_kv, pool_w, pool_scale, w_branch_pool, w_branch_attn, w_branch_mem, w_out, norm2_g, w_up, conv_w, conv_b, w_down):
    b, s, _ = x_prompt.shape
    db, t, _ = x_sample.shape
    assert t == 1 and norm1_g.shape[0] == 1 and s % MOBA_BLOCK == 0
    n_pages = page_table.shape[1]
    assert n_pages % PAGES_PER_BLOCK == 0
    m_tok = mem_prompt.shape[1]

    w_in_b = w_in[0].astype(BF16)
    wbp_b, wba_b, wbm_b = (w[0].astype(BF16) for w in (w_branch_pool, w_branch_attn, w_branch_mem))
    wo_b = w_out[0].astype(BF16)
    w_mem_b, pool_w_b = w_mem_kv[0].astype(BF16), pool_w[0].astype(BF16)
    g1, g2 = norm1_g, norm2_g
    conv_b2 = conv_b
    norm_args = (q_norm_g, k_norm_g, mem_q_norm_g, pool_w_b, pool_scale)

    n = b * s
    x2d = x_prompt.reshape(n, D_MODEL)
    tm = _row_tile(s, 512)
    tables_p = _rope_tables(jnp.arange(s, dtype=jnp.int32))
    u_p, q_p, k_p, v_p, qm_p, yp_p, hb_p = _inproj_prompt(x2d, g1, w_in_b, tables_p, *norm_args,
                                                          batch=b, seq=s, tm=tm)
    mk2d, mv2d = _mem_kv(mem_prompt.reshape(b * m_tok, D_MODEL), w_mem_b, mem_k_norm_g, tm=m_tok)
    ya_p, w_up_b, w_down_b = _moba_prompt(q_p, k_p, v_p, cast=(w_up[0], w_down[0]))
    mk_p = mk2d.reshape(b, m_tok, MEM_HEADS, HEAD_DIM)
    mv_p = mv2d.reshape(b, m_tok, MEM_HEADS, HEAD_DIM)
    ym_p = _mem_attend(qm_p, mk2d.reshape(b, m_tok, D_MEM), mv2d.reshape(b, m_tok, D_MEM),
                       rows_per_seq=s, tq=_row_tile(s, 1024))
    x1_p = _merge(x2d, hb_p, yp_p, ya_p, ym_p, w_in_b, wbp_b, wba_b, wbm_b, wo_b, tm=tm)
    y_p, sta_p, stg_p, sums = _ffn_seq(x1_p, g2, w_up_b, conv_w[0], conv_b2, w_down_b, page_table, cache_k,
                                       tm=tm, batch=b, seq=s)

    xs2d = x_sample.reshape(db, D_MODEL)
    past_len = n_pages * PAGE_SIZE
    pos_s = jnp.full((db,), past_len, jnp.int32)
    u_s, q_s, k_s, v_s, qm_s, yp_s, hb_s = _inproj_sample(xs2d, g1, w_in_b, _rope_tables(pos_s), *norm_args,
                                                          state_pool[0], past_len=past_len)
    q3, k3, v3 = (a.reshape(db, 1, D_ATTN) for a in (q_s, k_s, v_s))
    sel = _block_gate(q3, sums, n_pages)[:, :, :MOBA_TOPK].reshape(db, N_HEADS * MOBA_TOPK)
    ya_s = _moba_sample(page_table, sel, q3, k3, v3, cache_k, cache_v).reshape(db, D_ATTN)
    ym_s = _mem_attend_token(qm_s, cache_mem_k[0], cache_mem_v[0])
    x1_s = _merge(xs2d, hb_s, yp_s, ya_s, ym_s, w_in_b, wbp_b, wba_b, wbm_b, wo_b, tm=db)
    y_s, ua_s, ug_s = _ffn_step(x1_s, g2, w_up_b, conv_w[0], conv_b2, w_down_b,
                                state_conv[0, :, 0], state_conv[0, :, 1], tm=db)

    up_s = jnp.concatenate([ua_s, ug_s], axis=-1)
    return (
        y_p.reshape(b, s, D_MODEL),
        y_s.reshape(db, 1, D_MODEL),
        k_p[None], v_p[None],
        k_s.reshape(1, db, N_HEADS, 1, HEAD_DIM), v_s.reshape(1, db, N_HEADS, 1, HEAD_DIM),
        mk_p[None], mv_p[None],
        u_p.reshape(b, s, D_POOL)[None, :, s - POOL_STATE:],
        jnp.concatenate([state_pool[0][:, 1:], u_s[:, None]], axis=1)[None],
        jnp.concatenate([sta_p, stg_p], axis=-1)[None],
        jnp.concatenate([state_conv[0][:, 1:], up_s[:, None]], axis=1)[None],
    )
```
